```python
import jax, jax.numpy as jnp
from jax import lax
import numpy as np

D_MODEL = 1024
BATCH = 8
SEQ = 4096
DEPTH = 1

CHUNK = 128
RET_HEADS = 4
RET_DK = 128
RET_DV = 256
RET_QK_W = RET_HEADS * RET_DK
RET_V_W = RET_HEADS * RET_DV
SGU_GROUPS = 4
SGU_W = 1024
SGU_GW = SGU_W // SGU_GROUPS
D_FF = 2816
CONV_W = 3
ROPE_BASE = 10000.0
EPS = 1e-6
IN_SPLITS = (RET_QK_W, RET_QK_W, RET_V_W, RET_V_W, SGU_W, SGU_W, D_MODEL, D_MODEL)
IN_W = sum(IN_SPLITS)

kernel_name = "hybrid_retention_sgu_convffn_block"


def _rmsnorm(x, g):
    xf = x.astype(jnp.float32)
    y = xf * lax.rsqrt(jnp.mean(xf * xf, axis=-1, keepdims=True) + EPS)
    return (y * g.astype(jnp.float32)).astype(x.dtype)


def _layernorm(x, g=None, b=None):
    xf = x.astype(jnp.float32)
    mu = jnp.mean(xf, axis=-1, keepdims=True)
    var = jnp.mean(jnp.square(xf - mu), axis=-1, keepdims=True)
    y = (xf - mu) * lax.rsqrt(var + EPS)
    if g is not None:
        y = y * g.astype(jnp.float32) + b.astype(jnp.float32)
    return y.astype(x.dtype)


def _rotary(t):
    S, d = t.shape[1], t.shape[-1]
    half = d // 2
    inv_freq = 1.0 / (ROPE_BASE ** (jnp.arange(half, dtype=jnp.float32) / half))
    ang = jnp.arange(S, dtype=jnp.float32)[:, None] * inv_freq[None, :]
    cos = jnp.cos(ang)[None, :, None, :].astype(t.dtype)
    sin = jnp.sin(ang)[None, :, None, :].astype(t.dtype)
    t1, t2 = t[..., :half], t[..., half:]
    return jnp.concatenate([t1 * cos - t2 * sin, t1 * sin + t2 * cos], axis=-1)


def _to_chunks(t):
    B, S, H, d = t.shape
    return t.reshape(B, S // CHUNK, CHUNK, H, d).transpose(0, 3, 1, 2, 4)


def _from_chunks(t):
    B, H, N, C, d = t.shape
    return t.transpose(0, 2, 3, 1, 4).reshape(B, N * C, H, d)


def _retention_one_direction(q, k, v, log_gamma, strict):
    dt = q.dtype
    idx = jnp.arange(CHUNK, dtype=jnp.float32)
    diff = idx[:, None] - idx[None, :]
    mask = (diff > 0) if strict else (diff >= 0)
    lg = log_gamma[:, None, None]
    dmat = jnp.where(mask[None], jnp.exp(jnp.where(mask, diff, 0.0)[None] * lg), 0.0)
    s = jnp.einsum('bhncd,bhnmd->bhncm', q, k) * dmat[None, :, None].astype(dt)
    intra = jnp.einsum('bhncm,bhnme->bhnce', s, v)
    zeta = jnp.exp((CHUNK - 1.0 - idx)[None, :] * log_gamma[:, None]).astype(dt)
    xi = jnp.exp((idx + 1.0)[None, :] * log_gamma[:, None]).astype(dt)
    decay_c = jnp.exp(CHUNK * log_gamma).astype(dt)[None, :, None, None]
    kv = jnp.einsum('bhnmd,bhnme->nbhde', k, v * zeta[None, :, None, :, None])

    def step(state, kv_n):
        return decay_c * state + kv_n, state

    _, r_prev = lax.scan(step, jnp.zeros_like(kv[0]), kv)
    cross = jnp.einsum('bhncd,nbhde->bhnce', q * xi[None, :, None, :, None], r_prev)
    return intra + cross


def _retention_branch(q, k, v, g, ret_decay_logit, w_ret_o):
    B, S, _ = q.shape
    q = _rotary(q.reshape(B, S, RET_HEADS, RET_DK))
    k = _rotary(k.reshape(B, S, RET_HEADS, RET_DK)) * (RET_DK ** -0.5)
    v = v.reshape(B, S, RET_HEADS, RET_DV)
    log_gamma = jax.nn.log_sigmoid(ret_decay_logit.astype(jnp.float32))
    fwd = _retention_one_direction(_to_chunks(q), _to_chunks(k), _to_chunks(v), log_gamma[0], False)
    qb, kb, vb = (jnp.flip(t, axis=1) for t in (q, k, v))
    bwd = _retention_one_direction(_to_chunks(qb), _to_chunks(kb), _to_chunks(vb), log_gamma[1], True)
    o = _from_chunks(fwd) + jnp.flip(_from_chunks(bwd), axis=1)
    o = _layernorm(o).reshape(B, S, RET_V_W)
    return (o * jax.nn.silu(g)) @ w_ret_o


def _sgu_branch(u, sv, sgu_ln_g, sgu_ln_b, w_spatial, b_spatial, w_sgu_o):
    B, S, _ = u.shape
    u = jax.nn.gelu(u)
    sv = _layernorm(jax.nn.gelu(sv), sgu_ln_g, sgu_ln_b)
    svc = sv.reshape(B, S // CHUNK, CHUNK, SGU_GROUPS, SGU_GW)
    mixed = jnp.einsum('gpq,bnqgc->bnpgc', w_spatial, svc) + b_spatial.T[None, None, :, :, None]
    y = u * mixed.reshape(B, S, SGU_W)
    return y @ w_sgu_o


def _conv_gated_mlp(h, w_up, conv_w, conv_b, w_down):
    hid = h @ w_up
    c = hid.shape[-1]
    hid = lax.conv_general_dilated(
        hid, conv_w.reshape(CONV_W, 1, c), window_strides=(1,),
        padding=((CONV_W // 2, CONV_W // 2),),
        dimension_numbers=('NWC', 'WIO', 'NWC'), feature_group_count=c) + conv_b
    a, b = jnp.split(hid, 2, axis=-1)
    return (jax.nn.gelu(a) * b) @ w_down


def setup_inputs(seed: int = 0) -> dict:
    key = jax.random.key(seed)
    ks = jax.random.split(key, 20)
    f32 = jnp.float32
    nrm = lambda k, shape, s: jax.random.normal(k, shape, f32) * s
    gamma0 = 1.0 - 2.0 ** (-5.0 - np.arange(RET_HEADS, dtype=np.float32))
    logit0 = jnp.asarray(np.log(gamma0) - np.log1p(-gamma0), f32)
    return {
        "x": nrm(ks[0], (BATCH, SEQ, D_MODEL), 1.0),
        "g_pre_mix": 1.0 + nrm(ks[1], (D_MODEL,), 0.05),
        "w_in": nrm(ks[2], (D_MODEL, IN_W), D_MODEL ** -0.5),
        "ret_decay_logit": logit0[None, :] + nrm(ks[3], (2, RET_HEADS), 0.1),
        "sgu_ln_g": 1.0 + nrm(ks[4], (SGU_W,), 0.05),
        "sgu_ln_b": nrm(ks[5], (SGU_W,), 0.02),
        "w_spatial": nrm(ks[6], (SGU_GROUPS, CHUNK, CHUNK), CHUNK ** -0.5),
        "b_spatial": nrm(ks[7], (SGU_GROUPS, CHUNK), 0.02),
        "w_ret_o": nrm(ks[8], (RET_V_W, D_MODEL), RET_V_W ** -0.5),
        "w_sgu_o": nrm(ks[9], (SGU_W, D_MODEL), SGU_W ** -0.5),
        "w_out": nrm(ks[10], (D_MODEL, D_MODEL), D_MODEL ** -0.5),
        "g_post_mix": 1.0 + nrm(ks[11], (D_MODEL,), 0.05),
        "g_pre_ffn": 1.0 + nrm(ks[12], (D_MODEL,), 0.05),
        "w_up": nrm(ks[13], (D_MODEL, 2 * D_FF), D_MODEL ** -0.5),
        "conv_w": nrm(ks[14], (CONV_W, 2 * D_FF), CONV_W ** -0.5),
        "conv_b": nrm(ks[15], (2 * D_FF,), 0.02),
        "w_down": nrm(ks[16], (D_FF, D_MODEL), D_FF ** -0.5),
        "g_post_ffn": 1.0 + nrm(ks[17], (D_MODEL,), 0.05),
    }


def reference(x, g_pre_mix, w_in, ret_decay_logit, sgu_ln_g, sgu_ln_b, w_spatial, b_spatial,
              w_ret_o, w_sgu_o, w_out, g_post_mix, g_pre_ffn, w_up, conv_w, conv_b, w_down,
              g_post_ffn):
    for _ in range(DEPTH):
        h = _rmsnorm(x, g_pre_mix)
        proj = h @ w_in
        q, k, v, g, u, sv, a_ret, a_sgu = jnp.split(proj, list(np.cumsum(IN_SPLITS)[:-1]), axis=-1)
        y_ret = _retention_branch(q, k, v, g, ret_decay_logit, w_ret_o)
        y_sgu = _sgu_branch(u, sv, sgu_ln_g, sgu_ln_b, w_spatial, b_spatial, w_sgu_o)
        merged = jax.nn.sigmoid(a_ret) * y_ret + jax.nn.sigmoid(a_sgu) * y_sgu
        x = x + _rmsnorm(merged @ w_out, g_post_mix)
        h2 = _rmsnorm(x, g_pre_ffn)
        x = x + _rmsnorm(_conv_gated_mlp(h2, w_up, conv_w, conv_b, w_down), g_post_ffn)
    return x
```

```python
import functools

import jax
import jax.numpy as jnp
from jax import lax
from jax.experimental import pallas as pl
from jax.experimental.pallas import tpu as pltpu

D_MODEL = 1024
CHUNK = 128
RET_HEADS = 4
RET_DK = 128
RET_DV = 256
RET_QK_W = RET_HEADS * RET_DK
RET_V_W = RET_HEADS * RET_DV
SGU_GROUPS = 4
SGU_W = 1024
SGU_GW = SGU_W // SGU_GROUPS
D_FF = 2816
CONV_W = 3
ROPE_BASE = 10000.0
EPS = 1e-6
IN_SPLITS = (RET_QK_W, RET_QK_W, RET_V_W, RET_V_W, SGU_W, SGU_W, D_MODEL, D_MODEL)
IN_W = sum(IN_SPLITS)
IN_OFFS = tuple(sum(IN_SPLITS[:i]) for i in range(len(IN_SPLITS)))

TOKEN_TILE = 512
CHUNKS_PER_TILE = TOKEN_TILE // CHUNK
HALO = 16
FF_BLOCK = 256
VMEM_LIMIT_BYTES = 56 * 1024 * 1024

F32 = jnp.float32
BF16 = jnp.bfloat16


def _dot(a, b):
    return jnp.dot(a, b, preferred_element_type=F32)


def _rms(x, g):
    return x * lax.rsqrt(jnp.mean(x * x, axis=-1, keepdims=True) + EPS) * g


def _const_spec(shape):
    nd = len(shape)
    return pl.BlockSpec(shape, lambda *_: (0,) * nd, pipeline_mode=pl.Buffered(1))


def _inproj_kernel(decb_ref, x_ref, gpre_ref, w_ref, cos_ref, sin_ref, kcos_ref, ksin_ref,
                   lng_ref, lnb_ref, zb_ref,
                   q_ref, k_ref, v_ref, gs_ref, u_ref, sv_ref, sr_ref, ss_ref, rb_ref,
                   kz_scr, state_ref):
    @pl.when(pl.program_id(1) == 0)
    def _():
        state_ref[...] = jnp.zeros_like(state_ref)

    h = _rms(x_ref[0], gpre_ref[...]).astype(BF16)

    def proj(idx):
        off = IN_OFFS[idx]
        return _dot(h, w_ref[:, off:off + IN_SPLITS[idx]])

    def rotary(t, c_ref, s_ref):
        outs = []
        for hh in range(RET_HEADS):
            th = t[:, hh * RET_DK:(hh + 1) * RET_DK]
            outs.append(th * c_ref[...] + pltpu.roll(th, RET_DK // 2, axis=1) * s_ref[...])
        return outs

    qs = rotary(proj(0), cos_ref, sin_ref)
    for hh in range(RET_HEADS):
        q_ref[0, :, hh * RET_DK:(hh + 1) * RET_DK] = qs[hh].astype(BF16)

    ks = rotary(proj(1), kcos_ref, ksin_ref)
    for hh in range(RET_HEADS):
        k_ref[0, :, hh * RET_DK:(hh + 1) * RET_DK] = ks[hh].astype(BF16)
        for c in range(CHUNKS_PER_TILE):
            rows = slice(c * CHUNK, (c + 1) * CHUNK)
            kz_scr[hh, rows, :] = ks[hh][rows, :] * zb_ref[hh]

    v_ref[0] = proj(2).astype(BF16)
    g = proj(3)
    gs_ref[0] = (g * jax.nn.sigmoid(g)).astype(BF16)
    u_ref[0] = jax.nn.gelu(proj(4)).astype(BF16)
    sv = jax.nn.gelu(proj(5))
    mu = jnp.mean(sv, axis=-1, keepdims=True)
    svc = sv - mu
    var = jnp.mean(svc * svc, axis=-1, keepdims=True)
    sv_ref[0] = (svc * lax.rsqrt(var + EPS) * lng_ref[...] + lnb_ref[...]).astype(BF16)
    sr_ref[0] = jax.nn.sigmoid(proj(6)).astype(BF16)
    ss_ref[0] = jax.nn.sigmoid(proj(7)).astype(BF16)

    for c in reversed(range(CHUNKS_PER_TILE)):
        rows = slice(c * CHUNK, (c + 1) * CHUNK)
        for hh in range(RET_HEADS):
            rb_ref[0, c, hh] = state_ref[hh].astype(BF16)
            kzt = kz_scr[hh, rows, :].T.astype(BF16)
            vh = v_ref[0, rows, hh * RET_DV:(hh + 1) * RET_DV]
            state_ref[hh] = decb_ref[hh] * state_ref[hh] + _dot(kzt, vh)


def _inproj_call(x, g_pre, w_in, cos_t, sin_t, kcos_t, ksin_t, ln_g, ln_b, zb, decb):
    B, S, _ = x.shape
    TM = TOKEN_TILE
    nT = S // TM
    N = S // CHUNK
    rev = lambda b, j: (b, nT - 1 - j, 0)
    rev_tab = lambda b, j: (nT - 1 - j, 0)

    def tok(width):
        return pl.BlockSpec((1, TM, width), rev)

    out_shapes = [jax.ShapeDtypeStruct((B, S, w), BF16) for w in IN_SPLITS]
    out_shapes.append(jax.ShapeDtypeStruct((B, N, RET_HEADS, RET_DK, RET_DV), BF16))
    out_specs = [tok(w) for w in IN_SPLITS]
    out_specs.append(pl.BlockSpec((1, CHUNKS_PER_TILE, RET_HEADS, RET_DK, RET_DV),
                                  lambda b, j: (b, nT - 1 - j, 0, 0, 0)))
    tab = pl.BlockSpec((TM, RET_DK), rev_tab)
    return pl.pallas_call(
        _inproj_kernel,
        grid=(B, nT),
        in_specs=[
            pl.BlockSpec(memory_space=pltpu.SMEM),
            tok(D_MODEL),
            _const_spec((1, D_MODEL)),
            _const_spec((D_MODEL, IN_W)),
            tab, tab, tab, tab,
            _const_spec((1, SGU_W)),
            _const_spec((1, SGU_W)),
            _const_spec((RET_HEADS, CHUNK, RET_DK)),
        ],
        out_specs=out_specs,
        out_shape=out_shapes,
        scratch_shapes=[
            pltpu.VMEM((RET_HEADS, TM, RET_DK), F32),
            pltpu.VMEM((RET_HEADS, RET_DK, RET_DV), F32),
        ],
        compiler_params=pltpu.CompilerParams(
            dimension_semantics=("arbitrary", "arbitrary"),
            vmem_limit_bytes=VMEM_LIMIT_BYTES),
        name="inproj",
    )(decb, x, g_pre, w_in, cos_t, sin_t, kcos_t, ksin_t, ln_g, ln_b, zb)


def _mixer_kernel(decf_ref, x_ref, q_ref, k_ref, v_ref, gs_ref, u_ref, sv_ref, sr_ref, ss_ref,
                  rb_ref, dmat_ref, xif_ref, xib_ref, zf_ref, wsp_ref, bsp_ref,
                  wro_ref, wso_ref, wout_ref, gpm_ref, gpf_ref,
                  x1_ref, h2_ref,
                  o_scr, y_scr, rf_scr):
    @pl.when(pl.program_id(1) == 0)
    def _():
        rf_scr[...] = jnp.zeros_like(rf_scr)

    for c in range(CHUNKS_PER_TILE):
        rows = slice(c * CHUNK, (c + 1) * CHUNK)
        for hh in range(RET_HEADS):
            kc = slice(hh * RET_DK, (hh + 1) * RET_DK)
            vc = slice(hh * RET_DV, (hh + 1) * RET_DV)
            qh = q_ref[0, rows, kc]
            kh = k_ref[0, rows, kc]
            vh = v_ref[0, rows, vc]
            s = lax.dot_general(qh, kh, (((1,), (1,)), ((), ())), preferred_element_type=F32)
            p = (s * dmat_ref[hh]).astype(BF16)
            qf = qh.astype(F32)
            qxf = (qf * xif_ref[hh]).astype(BF16)
            qxb = (qf * xib_ref[hh]).astype(BF16)
            rf = rf_scr[hh]
            o = _dot(p, vh) + _dot(qxf, rf.astype(BF16)) + _dot(qxb, rb_ref[0, c, hh])
            kzt = (kh.astype(F32) * zf_ref[hh]).T.astype(BF16)
            rf_scr[hh] = decf_ref[hh] * rf + _dot(kzt, vh)
            mu = jnp.mean(o, axis=-1, keepdims=True)
            oc = o - mu
            var = jnp.mean(oc * oc, axis=-1, keepdims=True)
            on = oc * lax.rsqrt(var + EPS)
            o_scr[rows, vc] = (on * gs_ref[0, rows, vc].astype(F32)).astype(BF16)
        for g in range(SGU_GROUPS):
            gc = slice(g * SGU_GW, (g + 1) * SGU_GW)
            mixed = _dot(wsp_ref[g], sv_ref[0, rows, gc]) + bsp_ref[g]
            y_scr[rows, gc] = (u_ref[0, rows, gc].astype(F32) * mixed).astype(BF16)

    y_ret = _dot(o_scr[...], wro_ref[...])
    y_sgu = _dot(y_scr[...], wso_ref[...])
    merged = (sr_ref[0].astype(F32) * y_ret + ss_ref[0].astype(F32) * y_sgu).astype(BF16)
    x1 = x_ref[0] + _rms(_dot(merged, wout_ref[...]), gpm_ref[...])
    x1_ref[0] = x1
    h2_ref[0] = _rms(x1, gpf_ref[...]).astype(BF16)


def _mixer_call(x, q, k, v, gs, u, sv, sr, ss, rb, decf, dmat, xif, xib, zf, wsp, bsp,
                w_ret_o, w_sgu_o, w_out, g_post_mix, g_pre_ffn):
    B, S, _ = x.shape
    TM = TOKEN_TILE
    nT = S // TM
    fwd = lambda b, j: (b, j, 0)

    def tok(width):
        return pl.BlockSpec((1, TM, width), fwd)

    head_tab = _const_spec((RET_HEADS, CHUNK, RET_DK))
    return pl.pallas_call(
        _mixer_kernel,
        grid=(B, nT),
        in_specs=[
            pl.BlockSpec(memory_space=pltpu.SMEM),
            tok(D_MODEL),
            tok(RET_QK_W), tok(RET_QK_W), tok(RET_V_W), tok(RET_V_W),
            tok(SGU_W), tok(SGU_W), tok(D_MODEL), tok(D_MODEL),
            pl.BlockSpec((1, CHUNKS_PER_TILE, RET_HEADS, RET_DK, RET_DV),
                         lambda b, j: (b, j, 0, 0, 0)),
            head_tab, head_tab, head_tab, head_tab,
            _const_spec((SGU_GROUPS, CHUNK, CHUNK)),
            _const_spec((SGU_GROUPS, CHUNK, SGU_GW)),
            _const_spec((RET_V_W, D_MODEL)),
            _const_spec((SGU_W, D_MODEL)),
            _const_spec((D_MODEL, D_MODEL)),
            _const_spec((1, D_MODEL)),
            _const_spec((1, D_MODEL)),
        ],
        out_specs=[tok(D_MODEL), tok(D_MODEL)],
        out_shape=[jax.ShapeDtypeStruct((B, S, D_MODEL), F32),
                   jax.ShapeDtypeStruct((B, S, D_MODEL), BF16)],
        scratch_shapes=[
            pltpu.VMEM((TM, RET_V_W), BF16),
            pltpu.VMEM((TM, SGU_W), BF16),
            pltpu.VMEM((RET_HEADS, RET_DK, RET_DV), F32),
        ],
        compiler_params=pltpu.CompilerParams(
            dimension_semantics=("arbitrary", "arbitrary"),
            vmem_limit_bytes=VMEM_LIMIT_BYTES),
        name="mixer",
    )(decf, x, q, k, v, gs, u, sv, sr, ss, rb, dmat, xif, xib, zf, wsp, bsp,
      w_ret_o, w_sgu_o, w_out, g_post_mix, g_pre_ffn)


def _ffn_kernel(x1_ref, h2_ref, prev_ref, next_ref, wup_ref, cw_ref, cb_ref, wdn_ref, gpost_ref,
                out_ref, lhs_scr, act_scr):
    TM = TOKEN_TILE
    rows_all = TM + 2 * HALO
    j = pl.program_id(1)
    last = pl.num_programs(1) - 1
    lhs_scr[0:HALO, :] = jnp.where(j == 0, jnp.zeros_like(prev_ref[0]), prev_ref[0])
    lhs_scr[HALO:HALO + TM, :] = h2_ref[0]
    lhs_scr[HALO + TM:rows_all, :] = jnp.where(j == last, jnp.zeros_like(next_ref[0]), next_ref[0])
    lhs = lhs_scr[...]

    def conv(hid, col):
        w = cw_ref[:, col:col + FF_BLOCK]
        b = cb_ref[:, col:col + FF_BLOCK]
        dn = pltpu.roll(hid, 1, axis=0)[HALO:HALO + TM]
        up = pltpu.roll(hid, rows_all - 1, axis=0)[HALO:HALO + TM]
        return w[0:1] * dn + w[1:2] * hid[HALO:HALO + TM] + w[2:3] * up + b

    for fb in range(D_FF // FF_BLOCK):
        ca = conv(_dot(lhs, wup_ref[:, fb * FF_BLOCK:(fb + 1) * FF_BLOCK]), fb * FF_BLOCK)
        cb = conv(_dot(lhs, wup_ref[:, D_FF + fb * FF_BLOCK:D_FF + (fb + 1) * FF_BLOCK]),
                  D_FF + fb * FF_BLOCK)
        act_scr[:, fb * FF_BLOCK:(fb + 1) * FF_BLOCK] = (jax.nn.gelu(ca) * cb).astype(BF16)

    out_ref[0] = x1_ref[0] + _rms(_dot(act_scr[...], wdn_ref[...]), gpost_ref[...])


def _ffn_call(x1, h2, w_up, conv_w, conv_b, w_down, g_post_ffn):
    B, S, _ = x1.shape
    TM = TOKEN_TILE
    nT = S // TM
    hpt = TM // HALO
    n_halo = S // HALO
    tok = lambda w: pl.BlockSpec((1, TM, w), lambda b, j: (b, j, 0))
    return pl.pallas_call(
        _ffn_kernel,
        grid=(B, nT),
        in_specs=[
            tok(D_MODEL),
            tok(D_MODEL),
            pl.BlockSpec((1, HALO, D_MODEL), lambda b, j: (b, jnp.maximum(j * hpt - 1, 0), 0)),
            pl.BlockSpec((1, HALO, D_MODEL),
                         lambda b, j: (b, jnp.minimum((j + 1) * hpt, n_halo - 1), 0)),
            _const_spec((D_MODEL, 2 * D_FF)),
            _const_spec((CONV_W, 2 * D_FF)),
            _const_spec((1, 2 * D_FF)),
            _const_spec((D_FF, D_MODEL)),
            _const_spec((1, D_MODEL)),
        ],
        out_specs=tok(D_MODEL),
        out_shape=jax.ShapeDtypeStruct((B, S, D_MODEL), F32),
        scratch_shapes=[
            pltpu.VMEM((TM + 2 * HALO, D_MODEL), BF16),
            pltpu.VMEM((TM, D_FF), BF16),
        ],
        compiler_params=pltpu.CompilerParams(
            dimension_semantics=("arbitrary", "arbitrary"),
            vmem_limit_bytes=VMEM_LIMIT_BYTES),
        name="ffn",
    )(x1, h2, h2, h2, w_up, conv_w, conv_b, w_down, g_post_ffn)


def _rotary_tables(S):
    half = RET_DK // 2
    inv_freq = 1.0 / (ROPE_BASE ** (jnp.arange(half, dtype=F32) / half))
    ang = jnp.arange(S, dtype=F32)[:, None] * inv_freq[None, :]
    cos, sin = jnp.cos(ang), jnp.sin(ang)
    cos_t = jnp.concatenate([cos, cos], axis=-1)
    sin_t = jnp.concatenate([-sin, sin], axis=-1)
    scale = RET_DK ** -0.5
    return cos_t, sin_t, cos_t * scale, sin_t * scale


def _decay_tables(ret_decay_logit):
    lg = jax.nn.log_sigmoid(ret_decay_logit.astype(F32))
    lgf, lgb = lg[0][:, None, None], lg[1][:, None, None]
    idx = jnp.arange(CHUNK, dtype=F32)
    diff = idx[:, None] - idx[None, :]
    dmat = jnp.where(diff >= 0, jnp.exp(jnp.maximum(diff, 0.0)[None] * lgf),
                     jnp.exp(jnp.maximum(-diff, 0.0)[None] * lgb))
    col = jnp.broadcast_to(idx[None, :, None], (RET_HEADS, CHUNK, RET_DK))
    xif = jnp.exp((col + 1.0) * lgf)
    xib = jnp.exp((CHUNK - col) * lgb)
    zf = jnp.exp((CHUNK - 1.0 - col) * lgf)
    zb = jnp.exp(col * lgb)
    decf = jnp.exp(CHUNK * lg[0])
    decb = jnp.exp(CHUNK * lg[1])
    return dmat, xif, xib, zf, zb, decf, decb


def kernel(x, g_pre_mix, w_in, ret_decay_logit, sgu_ln_g, sgu_ln_b, w_spatial, b_spatial,
           w_ret_o, w_sgu_o, w_out, g_post_mix, g_pre_ffn, w_up, conv_w, conv_b, w_down,
           g_post_ffn):
    B, S, D = x.shape
    assert D == D_MODEL and S % TOKEN_TILE == 0
    row = lambda a: a.astype(F32).reshape(1, -1)
    cos_t, sin_t, kcos_t, ksin_t = _rotary_tables(S)
    dmat, xif, xib, zf, zb, decf, decb = _decay_tables(ret_decay_logit)
    bsp = jnp.broadcast_to(b_spatial.astype(F32)[:, :, None], (SGU_GROUPS, CHUNK, SGU_GW))

    q, k, v, gs, u, sv, sr, ss, rb = _inproj_call(
        x, row(g_pre_mix), w_in.astype(BF16), cos_t, sin_t, kcos_t, ksin_t,
        row(sgu_ln_g), row(sgu_ln_b), zb, decb)
    x1, h2 = _mixer_call(
        x, q, k, v, gs, u, sv, sr, ss, rb, decf, dmat, xif, xib, zf,
        w_spatial.astype(BF16), bsp, w_ret_o.astype(BF16), w_sgu_o.astype(BF16),
        w_out.astype(BF16), row(g_post_mix), row(g_pre_ffn))
    return _ffn_call(x1, h2, w_up.astype(BF16), conv_w.astype(F32), row(conv_b),
                     w_down.astype(BF16), row(g_post_ffn))
```

```python
import math

import jax
import jax.numpy as jnp
from jax import lax
from jax.experimental import pallas as pl
from jax.experimental.pallas import tpu as pltpu

D_MODEL = 1024
CHUNK = 128
RET_HEADS = 4
RET_DK = 128
RET_DV = 256
RET_QK_W = RET_HEADS * RET_DK
RET_V_W = RET_HEADS * RET_DV
SGU_GROUPS = 4
SGU_W = 1024
SGU_GW = SGU_W // SGU_GROUPS
D_FF = 2816
CONV_W = 3
ROPE_BASE = 10000.0
EPS = 1e-6
IN_SPLITS = (RET_QK_W, RET_QK_W, RET_V_W, RET_V_W, SGU_W, SGU_W, D_MODEL, D_MODEL)
IN_W = sum(IN_SPLITS)
IN_OFFS = tuple(sum(IN_SPLITS[:i]) for i in range(len(IN_SPLITS)))

TOKEN_TILE = 512
CHUNKS_PER_TILE = TOKEN_TILE // CHUNK
HALO = 16
FF_BLOCK = 256
GELU_C1 = math.sqrt(2.0 / math.pi)
GELU_C2 = GELU_C1 * 0.044715
VMEM_LIMIT_BYTES = 56 * 1024 * 1024

F32 = jnp.float32
BF16 = jnp.bfloat16


def _dot(a, b):
    return jnp.dot(a, b, preferred_element_type=F32)


def _rms(x, g):
    return x * lax.rsqrt(jnp.mean(x * x, axis=-1, keepdims=True) + EPS) * g


def _sigmoid(x):
    return 0.5 * jnp.tanh(0.5 * x) + 0.5


def _gelu(x):
    hx = 0.5 * x
    return hx + hx * jnp.tanh(x * (GELU_C1 + GELU_C2 * (x * x)))


def _const_spec(shape):
    nd = len(shape)
    return pl.BlockSpec(shape, lambda *_: (0,) * nd, pipeline_mode=pl.Buffered(1))


def _inproj_kernel(decb_ref, x_ref, gpre_ref, w_ref, cos_ref, sin_ref, kcos_ref, ksin_ref,
                   lng_ref, lnb_ref, zb_ref,
                   q_ref, k_ref, v_ref, gs_ref, u_ref, sv_ref, sr_ref, ss_ref, rb_ref,
                   kz_scr, state_ref):
    @pl.when(pl.program_id(1) == 0)
    def _():
        state_ref[...] = jnp.zeros_like(state_ref)

    h = _rms(x_ref[0], gpre_ref[...]).astype(BF16)

    def proj(idx):
        off = IN_OFFS[idx]
        return _dot(h, w_ref[:, off:off + IN_SPLITS[idx]])

    def rotary(t, c_ref, s_ref):
        outs = []
        for hh in range(RET_HEADS):
            th = t[:, hh * RET_DK:(hh + 1) * RET_DK]
            outs.append(th * c_ref[...] + pltpu.roll(th, RET_DK // 2, axis=1) * s_ref[...])
        return outs

    ks = rotary(proj(1), kcos_ref, ksin_ref)
    for hh in range(RET_HEADS):
        k_ref[0, :, hh * RET_DK:(hh + 1) * RET_DK] = ks[hh].astype(BF16)
        for c in range(CHUNKS_PER_TILE):
            rows = slice(c * CHUNK, (c + 1) * CHUNK)
            kz_scr[hh, rows, :] = ks[hh][rows, :] * zb_ref[hh]
    v_ref[0] = proj(2).astype(BF16)

    for c in reversed(range(CHUNKS_PER_TILE)):
        rows = slice(c * CHUNK, (c + 1) * CHUNK)
        for hh in range(RET_HEADS):
            rb_ref[0, c, hh] = state_ref[hh].astype(BF16)
            kzt = kz_scr[hh, rows, :].T.astype(BF16)
            vh = v_ref[0, rows, hh * RET_DV:(hh + 1) * RET_DV]
            state_ref[hh] = decb_ref[hh] * state_ref[hh] + _dot(kzt, vh)

    sv = _gelu(proj(5))
    mu = jnp.mean(sv, axis=-1, keepdims=True)
    svc = sv - mu
    var = jnp.mean(svc * svc, axis=-1, keepdims=True)
    sv_ref[0] = (svc * lax.rsqrt(var + EPS) * lng_ref[...] + lnb_ref[...]).astype(BF16)

    qs = rotary(proj(0), cos_ref, sin_ref)
    for hh in range(RET_HEADS):
        q_ref[0, :, hh * RET_DK:(hh + 1) * RET_DK] = qs[hh].astype(BF16)
    hg = 0.5 * proj(3)
    gs_ref[0] = (hg + hg * jnp.tanh(hg)).astype(BF16)
    u_ref[0] = _gelu(proj(4)).astype(BF16)
    sr_ref[0] = _sigmoid(proj(6)).astype(BF16)
    ss_ref[0] = _sigmoid(proj(7)).astype(BF16)


def _inproj_call(x, g_pre, w_in, cos_t, sin_t, kcos_t, ksin_t, ln_g, ln_b, zb, decb):
    B, S, _ = x.shape
    TM = TOKEN_TILE
    nT = S // TM
    N = S // CHUNK
    rev = lambda b, j: (b, nT - 1 - j, 0)
    rev_tab = lambda b, j: (nT - 1 - j, 0)

    def tok(width):
        return pl.BlockSpec((1, TM, width), rev)

    out_shapes = [jax.ShapeDtypeStruct((B, S, w), BF16) for w in IN_SPLITS]
    out_shapes.append(jax.ShapeDtypeStruct((B, N, RET_HEADS, RET_DK, RET_DV), BF16))
    out_specs = [tok(w) for w in IN_SPLITS]
    out_specs.append(pl.BlockSpec((1, CHUNKS_PER_TILE, RET_HEADS, RET_DK, RET_DV),
                                  lambda b, j: (b, nT - 1 - j, 0, 0, 0)))
    tab = pl.BlockSpec((TM, RET_DK), rev_tab)
    return pl.pallas_call(
        _inproj_kernel,
        grid=(B, nT),
        in_specs=[
            pl.BlockSpec(memory_space=pltpu.SMEM),
            tok(D_MODEL),
            _const_spec((1, D_MODEL)),
            _const_spec((D_MODEL, IN_W)),
            tab, tab, tab, tab,
            _const_spec((1, SGU_W)),
            _const_spec((1, SGU_W)),
            _const_spec((RET_HEADS, CHUNK, RET_DK)),
        ],
        out_specs=out_specs,
        out_shape=out_shapes,
        scratch_shapes=[
            pltpu.VMEM((RET_HEADS, TM, RET_DK), F32),
            pltpu.VMEM((RET_HEADS, RET_DK, RET_DV), F32),
        ],
        compiler_params=pltpu.CompilerParams(
            dimension_semantics=("arbitrary", "arbitrary"),
            vmem_limit_bytes=VMEM_LIMIT_BYTES),
        name="inproj",
    )(decb, x, g_pre, w_in, cos_t, sin_t, kcos_t, ksin_t, ln_g, ln_b, zb)


def _mixer_kernel(decf_ref, x_ref, q_ref, k_ref, v_ref, gs_ref, u_ref, sv_ref, sr_ref, ss_ref,
                  rb_ref, dmat_ref, xif_ref, xib_ref, zf_ref, wsp_ref, bsp_ref,
                  wro_ref, wso_ref, wout_ref, gpm_ref, gpf_ref,
                  x1_ref, h2_ref,
                  o_scr, y_scr, rf_scr):
    @pl.when(pl.program_id(1) == 0)
    def _():
        rf_scr[...] = jnp.zeros_like(rf_scr)

    for c in range(CHUNKS_PER_TILE):
        rows = slice(c * CHUNK, (c + 1) * CHUNK)
        for hh in range(RET_HEADS):
            kc = slice(hh * RET_DK, (hh + 1) * RET_DK)
            vc = slice(hh * RET_DV, (hh + 1) * RET_DV)
            qh = q_ref[0, rows, kc]
            kh = k_ref[0, rows, kc]
            vh = v_ref[0, rows, vc]
            s = lax.dot_general(qh, kh, (((1,), (1,)), ((), ())), preferred_element_type=F32)
            p = (s * dmat_ref[hh]).astype(BF16)
            qf = qh.astype(F32)
            qxf = (qf * xif_ref[hh]).astype(BF16)
            qxb = (qf * xib_ref[hh]).astype(BF16)
            rf = rf_scr[hh]
            o = _dot(p, vh) + _dot(qxf, rf.astype(BF16)) + _dot(qxb, rb_ref[0, c, hh])
            kzt = (kh.astype(F32) * zf_ref[hh]).T.astype(BF16)
            rf_scr[hh] = decf_ref[hh] * rf + _dot(kzt, vh)
            mu = jnp.mean(o, axis=-1, keepdims=True)
            oc = o - mu
            var = jnp.mean(oc * oc, axis=-1, keepdims=True)
            on = oc * lax.rsqrt(var + EPS)
            o_scr[rows, vc] = (on * gs_ref[0, rows, vc].astype(F32)).astype(BF16)
        for g in range(SGU_GROUPS):
            gc = slice(g * SGU_GW, (g + 1) * SGU_GW)
            mixed = _dot(wsp_ref[g], sv_ref[0, rows, gc]) + bsp_ref[g]
            y_scr[rows, gc] = (u_ref[0, rows, gc].astype(F32) * mixed).astype(BF16)

    y_ret = _dot(o_scr[...], wro_ref[...])
    y_sgu = _dot(y_scr[...], wso_ref[...])
    merged = (sr_ref[0].astype(F32) * y_ret + ss_ref[0].astype(F32) * y_sgu).astype(BF16)
    x1 = x_ref[0] + _rms(_dot(merged, wout_ref[...]), gpm_ref[...])
    x1_ref[0] = x1
    h2_ref[0] = _rms(x1, gpf_ref[...]).astype(BF16)


def _mixer_call(x, q, k, v, gs, u, sv, sr, ss, rb, decf, dmat, xif, xib, zf, wsp, bsp,
                w_ret_o, w_sgu_o, w_out, g_post_mix, g_pre_ffn):
    B, S, _ = x.shape
    TM = TOKEN_TILE
    nT = S // TM
    fwd = lambda b, j: (b, j, 0)

    def tok(width):
        return pl.BlockSpec((1, TM, width), fwd)

    head_tab = _const_spec((RET_HEADS, CHUNK, RET_DK))
    return pl.pallas_call(
        _mixer_kernel,
        grid=(B, nT),
        in_specs=[
            pl.BlockSpec(memory_space=pltpu.SMEM),
            tok(D_MODEL),
            tok(RET_QK_W), tok(RET_QK_W), tok(RET_V_W), tok(RET_V_W),
            tok(SGU_W), tok(SGU_W), tok(D_MODEL), tok(D_MODEL),
            pl.BlockSpec((1, CHUNKS_PER_TILE, RET_HEADS, RET_DK, RET_DV),
                         lambda b, j: (b, j, 0, 0, 0)),
            head_tab, head_tab, head_tab, head_tab,
            _const_spec((SGU_GROUPS, CHUNK, CHUNK)),
            _const_spec((SGU_GROUPS, CHUNK, SGU_GW)),
            _const_spec((RET_V_W, D_MODEL)),
            _const_spec((SGU_W, D_MODEL)),
            _const_spec((D_MODEL, D_MODEL)),
            _const_spec((1, D_MODEL)),
            _const_spec((1, D_MODEL)),
        ],
        out_specs=[tok(D_MODEL), tok(D_MODEL)],
        out_shape=[jax.ShapeDtypeStruct((B, S, D_MODEL), F32),
                   jax.ShapeDtypeStruct((B, S, D_MODEL), BF16)],
        scratch_shapes=[
            pltpu.VMEM((TM, RET_V_W), BF16),
            pltpu.VMEM((TM, SGU_W), BF16),
            pltpu.VMEM((RET_HEADS, RET_DK, RET_DV), F32),
        ],
        compiler_params=pltpu.CompilerParams(
            dimension_semantics=("arbitrary", "arbitrary"),
            vmem_limit_bytes=VMEM_LIMIT_BYTES),
        name="mixer",
    )(decf, x, q, k, v, gs, u, sv, sr, ss, rb, dmat, xif, xib, zf, wsp, bsp,
      w_ret_o, w_sgu_o, w_out, g_post_mix, g_pre_ffn)


def _ffn_kernel(x1_ref, h2_ref, prev_ref, next_ref, wup_ref, cw_ref, cb_ref, wdn_ref, gpost_ref,
                out_ref, lhs_scr, act_scr):
    TM = TOKEN_TILE
    rows_all = TM + 2 * HALO
    j = pl.program_id(1)
    last = pl.num_programs(1) - 1
    lhs_scr[0:HALO, :] = jnp.where(j == 0, jnp.zeros_like(prev_ref[0]), prev_ref[0])
    lhs_scr[HALO:HALO + TM, :] = h2_ref[0]
    lhs_scr[HALO + TM:rows_all, :] = jnp.where(j == last, jnp.zeros_like(next_ref[0]), next_ref[0])
    lhs = lhs_scr[...]

    def conv(hid, col):
        w = cw_ref[:, col:col + FF_BLOCK]
        b = cb_ref[:, col:col + FF_BLOCK]
        dn = pltpu.roll(hid, 1, axis=0)[HALO:HALO + TM]
        up = pltpu.roll(hid, rows_all - 1, axis=0)[HALO:HALO + TM]
        return w[0:1] * dn + w[1:2] * hid[HALO:HALO + TM] + w[2:3] * up + b

    for fb in range(D_FF // FF_BLOCK):
        ca = conv(_dot(lhs, wup_ref[:, fb * FF_BLOCK:(fb + 1) * FF_BLOCK]), fb * FF_BLOCK)
        cbh = conv(_dot(lhs, wup_ref[:, D_FF + fb * FF_BLOCK:D_FF + (fb + 1) * FF_BLOCK]),
                   D_FF + fb * FF_BLOCK)
        inner = ca * (GELU_C1 + GELU_C2 * (ca * ca))
        act = (ca * cbh) * (1.0 + jnp.tanh(inner))
        act_scr[:, fb * FF_BLOCK:(fb + 1) * FF_BLOCK] = act.astype(BF16)

    out_ref[0] = x1_ref[0] + _rms(_dot(act_scr[...], wdn_ref[...]), gpost_ref[...])


def _ffn_call(x1, h2, w_up, conv_w, conv_b, w_down, g_post_ffn):
    B, S, _ = x1.shape
    TM = TOKEN_TILE
    nT = S // TM
    hpt = TM // HALO
    n_halo = S // HALO
    tok = lambda w: pl.BlockSpec((1, TM, w), lambda b, j: (b, j, 0))
    return pl.pallas_call(
        _ffn_kernel,
        grid=(B, nT),
        in_specs=[
            tok(D_MODEL),
            tok(D_MODEL),
            pl.BlockSpec((1, HALO, D_MODEL), lambda b, j: (b, jnp.maximum(j * hpt - 1, 0), 0)),
            pl.BlockSpec((1, HALO, D_MODEL),
                         lambda b, j: (b, jnp.minimum((j + 1) * hpt, n_halo - 1), 0)),
            _const_spec((D_MODEL, 2 * D_FF)),
            _const_spec((CONV_W, 2 * D_FF)),
            _const_spec((1, 2 * D_FF)),
            _const_spec((D_FF, D_MODEL)),
            _const_spec((1, D_MODEL)),
        ],
        out_specs=tok(D_MODEL),
        out_shape=jax.ShapeDtypeStruct((B, S, D_MODEL), F32),
        scratch_shapes=[
            pltpu.VMEM((TM + 2 * HALO, D_MODEL), BF16),
            pltpu.VMEM((TM, D_FF), BF16),
        ],
        compiler_params=pltpu.CompilerParams(
            dimension_semantics=("arbitrary", "arbitrary"),
            vmem_limit_bytes=VMEM_LIMIT_BYTES),
        name="ffn",
    )(x1, h2, h2, h2, w_up, conv_w, conv_b, w_down, g_post_ffn)


def _rotary_tables(S):
    half = RET_DK // 2
    inv_freq = 1.0 / (ROPE_BASE ** (jnp.arange(half, dtype=F32) / half))
    ang = jnp.arange(S, dtype=F32)[:, None] * inv_freq[None, :]
    cos, sin = jnp.cos(ang), jnp.sin(ang)
    cos_t = jnp.concatenate([cos, cos], axis=-1)
    sin_t = jnp.concatenate([-sin, sin], axis=-1)
    scale = RET_DK ** -0.5
    return cos_t, sin_t, cos_t * scale, sin_t * scale


def _decay_tables(ret_decay_logit):
    lg = jax.nn.log_sigmoid(ret_decay_logit.astype(F32))
    lgf, lgb = lg[0][:, None, None], lg[1][:, None, None]
    idx = jnp.arange(CHUNK, dtype=F32)
    diff = idx[:, None] - idx[None, :]
    dmat = jnp.where(diff >= 0, jnp.exp(jnp.maximum(diff, 0.0)[None] * lgf),
                     jnp.exp(jnp.maximum(-diff, 0.0)[None] * lgb))
    col = jnp.broadcast_to(idx[None, :, None], (RET_HEADS, CHUNK, RET_DK))
    xif = jnp.exp((col + 1.0) * lgf)
    xib = jnp.exp((CHUNK - col) * lgb)
    zf = jnp.exp((CHUNK - 1.0 - col) * lgf)
    zb = jnp.exp(col * lgb)
    decf = jnp.exp(CHUNK * lg[0])
    decb = jnp.exp(CHUNK * lg[1])
    return dmat, xif, xib, zf, zb, decf, decb


def kernel(x, g_pre_mix, w_in, ret_decay_logit, sgu_ln_g, sgu_ln_b, w_spatial, b_spatial,
           w_ret_o, w_sgu_o, w_out, g_post_mix, g_pre_ffn, w_up, conv_w, conv_b, w_down,
           g_post_ffn):
    B, S, D = x.shape
    assert D == D_MODEL and S % TOKEN_TILE == 0
    row = lambda a: a.astype(F32).reshape(1, -1)
    cos_t, sin_t, kcos_t, ksin_t = _rotary_tables(S)
    dmat, xif, xib, zf, zb, decf, decb = _decay_tables(ret_decay_logit)
    bsp = jnp.broadcast_to(b_spatial.astype(F32)[:, :, None], (SGU_GROUPS, CHUNK, SGU_GW))

    q, k, v, gs, u, sv, sr, ss, rb = _inproj_call(
        x, row(g_pre_mix), w_in.astype(BF16), cos_t, sin_t, kcos_t, ksin_t,
        row(sgu_ln_g), row(sgu_ln_b), zb, decb)
    x1, h2 = _mixer_call(
        x, q, k, v, gs, u, sv, sr, ss, rb, decf, dmat, xif, xib, zf,
        w_spatial.astype(BF16), bsp, w_ret_o.astype(BF16), w_sgu_o.astype(BF16),
        w_out.astype(BF16), row(g_post_mix), row(g_pre_ffn))
    gate_half = jnp.concatenate([jnp.ones((D_FF,), F32), jnp.full((D_FF,), 0.5, F32)])[None, :]
    return _ffn_call(x1, h2, w_up.astype(BF16), conv_w.astype(F32) * gate_half,
                     row(conv_b) * gate_half, w_down.astype(BF16), row(g_post_ffn))
```

```python
import math

import jax
import jax.numpy as jnp
from jax import lax
from jax.experimental import pallas as pl
from jax.experimental.pallas import tpu as pltpu

D_MODEL = 1024
CHUNK = 128
RET_HEADS = 4
RET_DK = 128
RET_DV = 256
RET_QK_W = RET_HEADS * RET_DK
RET_V_W = RET_HEADS * RET_DV
SGU_GROUPS = 4
SGU_W = 1024
SGU_GW = SGU_W // SGU_GROUPS
D_FF = 2816
CONV_W = 3
ROPE_BASE = 10000.0
EPS = 1e-6
IN_SPLITS = (RET_QK_W, RET_QK_W, RET_V_W, RET_V_W, SGU_W, SGU_W, D_MODEL, D_MODEL)
IN_W = sum(IN_SPLITS)
IN_OFFS = tuple(sum(IN_SPLITS[:i]) for i in range(len(IN_SPLITS)))

TOKEN_TILE = 512
CHUNKS_PER_TILE = TOKEN_TILE // CHUNK
DENSE_ROWS = 256
HALO = 16
FF_BLOCK = 256
GELU_C1 = math.sqrt(2.0 / math.pi)
GELU_C2 = GELU_C1 * 0.044715
VMEM_LIMIT_BYTES = 56 * 1024 * 1024

F32 = jnp.float32
BF16 = jnp.bfloat16


def _dot(a, b):
    return jnp.dot(a, b, preferred_element_type=F32)


def _rms(x, g):
    return x * lax.rsqrt(jnp.mean(x * x, axis=-1, keepdims=True) + EPS) * g


def _sigmoid(x):
    return 0.5 * jnp.tanh(0.5 * x) + 0.5


def _gelu(x):
    hx = 0.5 * x
    return hx + hx * jnp.tanh(x * (GELU_C1 + GELU_C2 * (x * x)))


def _const_spec(shape):
    nd = len(shape)
    return pl.BlockSpec(shape, lambda *_: (0,) * nd, pipeline_mode=pl.Buffered(1))


def _inproj_kernel(decb_ref, x_ref, gpre_ref, w_ref, cos_ref, sin_ref, kcos_ref, ksin_ref,
                   lng_ref, lnb_ref, zb_ref,
                   q_ref, k_ref, v_ref, gs_ref, u_ref, sv_ref, sr_ref, ss_ref, rb_ref,
                   kz_scr, state_ref):
    @pl.when(pl.program_id(1) == 0)
    def _():
        state_ref[...] = jnp.zeros_like(state_ref)

    h = _rms(x_ref[0], gpre_ref[...]).astype(BF16)

    def proj(idx):
        off = IN_OFFS[idx]
        return _dot(h, w_ref[:, off:off + IN_SPLITS[idx]])

    def rotary(t, c_ref, s_ref):
        outs = []
        for hh in range(RET_HEADS):
            th = t[:, hh * RET_DK:(hh + 1) * RET_DK]
            outs.append(th * c_ref[...] + pltpu.roll(th, RET_DK // 2, axis=1) * s_ref[...])
        return outs

    ks = rotary(proj(1), kcos_ref, ksin_ref)
    for hh in range(RET_HEADS):
        k_ref[0, :, hh * RET_DK:(hh + 1) * RET_DK] = ks[hh].astype(BF16)
        for c in range(CHUNKS_PER_TILE):
            rows = slice(c * CHUNK, (c + 1) * CHUNK)
            kz_scr[hh, rows, :] = ks[hh][rows, :] * zb_ref[hh]
    v_ref[0] = proj(2).astype(BF16)

    for c in reversed(range(CHUNKS_PER_TILE)):
        rows = slice(c * CHUNK, (c + 1) * CHUNK)
        for hh in range(RET_HEADS):
            rb_ref[0, c, hh] = state_ref[hh].astype(BF16)
            kzt = kz_scr[hh, rows, :].T.astype(BF16)
            vh = v_ref[0, rows, hh * RET_DV:(hh + 1) * RET_DV]
            state_ref[hh] = decb_ref[hh] * state_ref[hh] + _dot(kzt, vh)

    sv = _gelu(proj(5))
    mu = jnp.mean(sv, axis=-1, keepdims=True)
    svc = sv - mu
    var = jnp.mean(svc * svc, axis=-1, keepdims=True)
    sv_ref[0] = (svc * lax.rsqrt(var + EPS) * lng_ref[...] + lnb_ref[...]).astype(BF16)

    qs = rotary(proj(0), cos_ref, sin_ref)
    for hh in range(RET_HEADS):
        q_ref[0, :, hh * RET_DK:(hh + 1) * RET_DK] = qs[hh].astype(BF16)
    hg = 0.5 * proj(3)
    gs_ref[0] = (hg + hg * jnp.tanh(hg)).astype(BF16)
    u_ref[0] = _gelu(proj(4)).astype(BF16)
    sr_ref[0] = _sigmoid(proj(6)).astype(BF16)
    ss_ref[0] = _sigmoid(proj(7)).astype(BF16)


def _inproj_call(x, g_pre, w_in, cos_t, sin_t, kcos_t, ksin_t, ln_g, ln_b, zb, decb):
    B, S, _ = x.shape
    TM = TOKEN_TILE
    nT = S // TM
    N = S // CHUNK
    rev = lambda b, j: (b, nT - 1 - j, 0)
    rev_tab = lambda b, j: (nT - 1 - j, 0)

    def tok(width):
        return pl.BlockSpec((1, TM, width), rev)

    out_shapes = [jax.ShapeDtypeStruct((B, S, w), BF16) for w in IN_SPLITS]
    out_shapes.append(jax.ShapeDtypeStruct((B, N, RET_HEADS, RET_DK, RET_DV), BF16))
    out_specs = [tok(w) for w in IN_SPLITS]
    out_specs.append(pl.BlockSpec((1, CHUNKS_PER_TILE, RET_HEADS, RET_DK, RET_DV),
                                  lambda b, j: (b, nT - 1 - j, 0, 0, 0)))
    tab = pl.BlockSpec((TM, RET_DK), rev_tab)
    return pl.pallas_call(
        _inproj_kernel,
        grid=(B, nT),
        in_specs=[
            pl.BlockSpec(memory_space=pltpu.SMEM),
            tok(D_MODEL),
            _const_spec((1, D_MODEL)),
            _const_spec((D_MODEL, IN_W)),
            tab, tab, tab, tab,
            _const_spec((1, SGU_W)),
            _const_spec((1, SGU_W)),
            _const_spec((RET_HEADS, CHUNK, RET_DK)),
        ],
        out_specs=out_specs,
        out_shape=out_shapes,
        scratch_shapes=[
            pltpu.VMEM((RET_HEADS, TM, RET_DK), F32),
            pltpu.VMEM((RET_HEADS, RET_DK, RET_DV), F32),
        ],
        compiler_params=pltpu.CompilerParams(
            dimension_semantics=("arbitrary", "arbitrary"),
            vmem_limit_bytes=VMEM_LIMIT_BYTES),
        name="inproj",
    )(decb, x, g_pre, w_in, cos_t, sin_t, kcos_t, ksin_t, ln_g, ln_b, zb)


def _mixer_kernel(decf_ref, x_ref, q_ref, k_ref, v_ref, gs_ref, u_ref, sv_ref, sr_ref, ss_ref,
                  rb_ref, dmat_ref, xif_ref, xib_ref, zf_ref, wsp_ref, bsp_ref,
                  wro_ref, wso_ref, wout_ref, gpm_ref, gpf_ref,
                  x1_ref, h2_ref,
                  o_scr, y_scr, rf_scr):
    @pl.when(pl.program_id(1) == 0)
    def _():
        rf_scr[...] = jnp.zeros_like(rf_scr)

    heads = range(RET_HEADS)
    kcs = [slice(hh * RET_DK, (hh + 1) * RET_DK) for hh in heads]
    vcs = [slice(hh * RET_DV, (hh + 1) * RET_DV) for hh in heads]

    def token_mix(c):
        rows = slice(c * CHUNK, (c + 1) * CHUNK)
        qhs = [q_ref[0, rows, kcs[hh]] for hh in heads]
        khs = [k_ref[0, rows, kcs[hh]] for hh in heads]
        vhs = [v_ref[0, rows, vcs[hh]] for hh in heads]
        ss = [lax.dot_general(qhs[hh], khs[hh], (((1,), (1,)), ((), ())),
                              preferred_element_type=F32) for hh in heads]
        rfs = [rf_scr[hh] for hh in heads]
        kvs = [_dot((khs[hh].astype(F32) * zf_ref[hh]).T.astype(BF16), vhs[hh]) for hh in heads]
        outs = []
        for hh in heads:
            p = (ss[hh] * dmat_ref[hh]).astype(BF16)
            qf = qhs[hh].astype(F32)
            qxf = (qf * xif_ref[hh]).astype(BF16)
            qxb = (qf * xib_ref[hh]).astype(BF16)
            lhs = jnp.concatenate([p, qxf], axis=1)
            rhs = jnp.concatenate([vhs[hh], rfs[hh].astype(BF16)], axis=0)
            outs.append(_dot(lhs, rhs) + _dot(qxb, rb_ref[0, c, hh]))
        for hh in heads:
            rf_scr[hh] = decf_ref[hh] * rfs[hh] + kvs[hh]
        for hh in heads:
            o = outs[hh]
            mu = jnp.mean(o, axis=-1, keepdims=True)
            oc = o - mu
            var = jnp.mean(oc * oc, axis=-1, keepdims=True)
            on = oc * lax.rsqrt(var + EPS)
            o_scr[rows, vcs[hh]] = (on * gs_ref[0, rows, vcs[hh]].astype(F32)).astype(BF16)
        for g in range(SGU_GROUPS):
            gc = slice(g * SGU_GW, (g + 1) * SGU_GW)
            mixed = _dot(wsp_ref[g], sv_ref[0, rows, gc]) + bsp_ref[g]
            y_scr[rows, gc] = (u_ref[0, rows, gc].astype(F32) * mixed).astype(BF16)

    def dense(r0):
        rows = slice(r0, r0 + DENSE_ROWS)
        y_ret = _dot(o_scr[rows, :], wro_ref[...])
        y_sgu = _dot(y_scr[rows, :], wso_ref[...])
        merged = (sr_ref[0, rows, :].astype(F32) * y_ret
                  + ss_ref[0, rows, :].astype(F32) * y_sgu).astype(BF16)
        x1 = x_ref[0, rows, :] + _rms(_dot(merged, wout_ref[...]), gpm_ref[...])
        x1_ref[0, rows, :] = x1
        h2_ref[0, rows, :] = _rms(x1, gpf_ref[...]).astype(BF16)

    for c in range(CHUNKS_PER_TILE):
        token_mix(c)
    for r0 in range(0, TOKEN_TILE, DENSE_ROWS):
        dense(r0)


def _mixer_call(x, q, k, v, gs, u, sv, sr, ss, rb, decf, dmat, xif, xib, zf, wsp, bsp,
                w_ret_o, w_sgu_o, w_out, g_post_mix, g_pre_ffn):
    B, S, _ = x.shape
    TM = TOKEN_TILE
    nT = S // TM
    fwd = lambda b, j: (b, j, 0)

    def tok(width):
        return pl.BlockSpec((1, TM, width), fwd)

    head_tab = _const_spec((RET_HEADS, CHUNK, RET_DK))
    return pl.pallas_call(
        _mixer_kernel,
        grid=(B, nT),
        in_specs=[
            pl.BlockSpec(memory_space=pltpu.SMEM),
            tok(D_MODEL),
            tok(RET_QK_W), tok(RET_QK_W), tok(RET_V_W), tok(RET_V_W),
            tok(SGU_W), tok(SGU_W), tok(D_MODEL), tok(D_MODEL),
            pl.BlockSpec((1, CHUNKS_PER_TILE, RET_HEADS, RET_DK, RET_DV),
                         lambda b, j: (b, j, 0, 0, 0)),
            head_tab, head_tab, head_tab, head_tab,
            _const_spec((SGU_GROUPS, CHUNK, CHUNK)),
            _const_spec((SGU_GROUPS, CHUNK, SGU_GW)),
            _const_spec((RET_V_W, D_MODEL)),
            _const_spec((SGU_W, D_MODEL)),
            _const_spec((D_MODEL, D_MODEL)),
            _const_spec((1, D_MODEL)),
            _const_spec((1, D_MODEL)),
        ],
        out_specs=[tok(D_MODEL), tok(D_MODEL)],
        out_shape=[jax.ShapeDtypeStruct((B, S, D_MODEL), F32),
                   jax.ShapeDtypeStruct((B, S, D_MODEL), BF16)],
        scratch_shapes=[
            pltpu.VMEM((TM, RET_V_W), BF16),
            pltpu.VMEM((TM, SGU_W), BF16),
            pltpu.VMEM((RET_HEADS, RET_DK, RET_DV), F32),
        ],
        compiler_params=pltpu.CompilerParams(
            dimension_semantics=("arbitrary", "arbitrary"),
            vmem_limit_bytes=VMEM_LIMIT_BYTES),
        name="mixer",
    )(decf, x, q, k, v, gs, u, sv, sr, ss, rb, dmat, xif, xib, zf, wsp, bsp,
      w_ret_o, w_sgu_o, w_out, g_post_mix, g_pre_ffn)


def _ffn_kernel(x1_ref, h2_ref, prev_ref, next_ref, wup_ref, cw_ref, cb_ref, wdn_ref, gpost_ref,
                out_ref, lhs_scr, act_scr):
    TM = TOKEN_TILE
    rows_all = TM + 2 * HALO
    j = pl.program_id(1)
    last = pl.num_programs(1) - 1
    lhs_scr[0:HALO, :] = jnp.where(j == 0, jnp.zeros_like(prev_ref[0]), prev_ref[0])
    lhs_scr[HALO:HALO + TM, :] = h2_ref[0]
    lhs_scr[HALO + TM:rows_all, :] = jnp.where(j == last, jnp.zeros_like(next_ref[0]), next_ref[0])
    lhs = lhs_scr[...]

    def conv(hid, col):
        w = cw_ref[:, col:col + FF_BLOCK]
        b = cb_ref[:, col:col + FF_BLOCK]
        dn = pltpu.roll(hid, 1, axis=0)[HALO:HALO + TM]
        up = pltpu.roll(hid, rows_all - 1, axis=0)[HALO:HALO + TM]
        return w[0:1] * dn + w[1:2] * hid[HALO:HALO + TM] + w[2:3] * up + b

    for fb in range(D_FF // FF_BLOCK):
        ca = conv(_dot(lhs, wup_ref[:, fb * FF_BLOCK:(fb + 1) * FF_BLOCK]), fb * FF_BLOCK)
        cbh = conv(_dot(lhs, wup_ref[:, D_FF + fb * FF_BLOCK:D_FF + (fb + 1) * FF_BLOCK]),
                   D_FF + fb * FF_BLOCK)
        inner = ca * (GELU_C1 + GELU_C2 * (ca * ca))
        act = (ca * cbh) * (1.0 + jnp.tanh(inner))
        act_scr[:, fb * FF_BLOCK:(fb + 1) * FF_BLOCK] = act.astype(BF16)

    out_ref[0] = x1_ref[0] + _rms(_dot(act_scr[...], wdn_ref[...]), gpost_ref[...])


def _ffn_call(x1, h2, w_up, conv_w, conv_b, w_down, g_post_ffn):
    B, S, _ = x1.shape
    TM = TOKEN_TILE
    nT = S // TM
    hpt = TM // HALO
    n_halo = S // HALO
    tok = lambda w: pl.BlockSpec((1, TM, w), lambda b, j: (b, j, 0))
    return pl.pallas_call(
        _ffn_kernel,
        grid=(B, nT),
        in_specs=[
            tok(D_MODEL),
            tok(D_MODEL),
            pl.BlockSpec((1, HALO, D_MODEL), lambda b, j: (b, jnp.maximum(j * hpt - 1, 0), 0)),
            pl.BlockSpec((1, HALO, D_MODEL),
                         lambda b, j: (b, jnp.minimum((j + 1) * hpt, n_halo - 1), 0)),
            _const_spec((D_MODEL, 2 * D_FF)),
            _const_spec((CONV_W, 2 * D_FF)),
            _const_spec((1, 2 * D_FF)),
            _const_spec((D_FF, D_MODEL)),
            _const_spec((1, D_MODEL)),
        ],
        out_specs=tok(D_MODEL),
        out_shape=jax.ShapeDtypeStruct((B, S, D_MODEL), F32),
        scratch_shapes=[
            pltpu.VMEM((TM + 2 * HALO, D_MODEL), BF16),
            pltpu.VMEM((TM, D_FF), BF16),
        ],
        compiler_params=pltpu.CompilerParams(
            dimension_semantics=("arbitrary", "arbitrary"),
            vmem_limit_bytes=VMEM_LIMIT_BYTES),
        name="ffn",
    )(x1, h2, h2, h2, w_up, conv_w, conv_b, w_down, g_post_ffn)


def _rotary_tables(S):
    half = RET_DK // 2
    inv_freq = 1.0 / (ROPE_BASE ** (jnp.arange(half, dtype=F32) / half))
    ang = jnp.arange(S, dtype=F32)[:, None] * inv_freq[None, :]
    cos, sin = jnp.cos(ang), jnp.sin(ang)
    cos_t = jnp.concatenate([cos, cos], axis=-1)
    sin_t = jnp.concatenate([-sin, sin], axis=-1)
    scale = RET_DK ** -0.5
    return cos_t, sin_t, cos_t * scale, sin_t * scale


def _decay_tables(ret_decay_logit):
    lg = jax.nn.log_sigmoid(ret_decay_logit.astype(F32))
    lgf, lgb = lg[0][:, None, None], lg[1][:, None, None]
    idx = jnp.arange(CHUNK, dtype=F32)
    diff = idx[:, None] - idx[None, :]
    dmat = jnp.where(diff >= 0, jnp.exp(jnp.maximum(diff, 0.0)[None] * lgf),
                     jnp.exp(jnp.maximum(-diff, 0.0)[None] * lgb))
    col = jnp.broadcast_to(idx[None, :, None], (RET_HEADS, CHUNK, RET_DK))
    xif = jnp.exp((col + 1.0) * lgf)
    xib = jnp.exp((CHUNK - col) * lgb)
    zf = jnp.exp((CHUNK - 1.0 - col) * lgf)
    zb = jnp.exp(col * lgb)
    decf = jnp.exp(CHUNK * lg[0])
    decb = jnp.exp(CHUNK * lg[1])
    return dmat, xif, xib, zf, zb, decf, decb


def kernel(x, g_pre_mix, w_in, ret_decay_logit, sgu_ln_g, sgu_ln_b, w_spatial, b_spatial,
           w_ret_o, w_sgu_o, w_out, g_post_mix, g_pre_ffn, w_up, conv_w, conv_b, w_down,
           g_post_ffn):
    B, S, D = x.shape
    assert D == D_MODEL and S % TOKEN_TILE == 0
    row = lambda a: a.astype(F32).reshape(1, -1)
    cos_t, sin_t, kcos_t, ksin_t = _rotary_tables(S)
    dmat, xif, xib, zf, zb, decf, decb = _decay_tables(ret_decay_logit)
    bsp = jnp.broadcast_to(b_spatial.astype(F32)[:, :, None], (SGU_GROUPS, CHUNK, SGU_GW))

    q, k, v, gs, u, sv, sr, ss, rb = _inproj_call(
        x, row(g_pre_mix), w_in.astype(BF16), cos_t, sin_t, kcos_t, ksin_t,
        row(sgu_ln_g), row(sgu_ln_b), zb, decb)
    x1, h2 = _mixer_call(
        x, q, k, v, gs, u, sv, sr, ss, rb, decf, dmat, xif, xib, zf,
        w_spatial.astype(BF16), bsp, w_ret_o.astype(BF16), w_sgu_o.astype(BF16),
        w_out.astype(BF16), row(g_post_mix), row(g_pre_ffn))
    gate_half = jnp.concatenate([jnp.ones((D_FF,), F32), jnp.full((D_FF,), 0.5, F32)])[None, :]
    return _ffn_call(x1, h2, w_up.astype(BF16), conv_w.astype(F32) * gate_half,
                     row(conv_b) * gate_half, w_down.astype(BF16), row(g_post_ffn))
```

```python
import math

import jax
import jax.numpy as jnp
import numpy as np
from jax import lax
from jax.experimental import pallas as pl
from jax.experimental.pallas import tpu as pltpu

D_MODEL = 1024
CHUNK = 128
RET_HEADS = 4
RET_DK = 128
RET_DV = 256
RET_QK_W = RET_HEADS * RET_DK
RET_V_W = RET_HEADS * RET_DV
SGU_GROUPS = 4
SGU_W = 1024
SGU_GW = SGU_W // SGU_GROUPS
D_FF = 2816
CONV_W = 3
ROPE_BASE = 10000.0
EPS = 1e-6
IN_SPLITS = (RET_QK_W, RET_QK_W, RET_V_W, RET_V_W, SGU_W, SGU_W, D_MODEL, D_MODEL)
IN_W = sum(IN_SPLITS)
IN_OFFS = tuple(sum(IN_SPLITS[:i]) for i in range(len(IN_SPLITS)))

TOKEN_TILE = 512
CHUNKS_PER_TILE = TOKEN_TILE // CHUNK
FFN_TILE = 1024
CAST_ROW_ALIGN = 16
DENSE_ROWS = 256
HALO = 16
FF_BLOCK = 256
GELU_C1 = math.sqrt(2.0 / math.pi)
GELU_C2 = GELU_C1 * 0.044715
VMEM_LIMIT_BYTES = 56 * 1024 * 1024

F32 = jnp.float32
BF16 = jnp.bfloat16


def _dot(a, b):
    return jnp.dot(a, b, preferred_element_type=F32)


def _rms(x, g):
    return x * lax.rsqrt(jnp.mean(x * x, axis=-1, keepdims=True) + EPS) * g


def _sigmoid(x):
    return 0.5 * jnp.tanh(0.5 * x) + 0.5


def _gelu(x):
    hx = 0.5 * x
    return hx + hx * jnp.tanh(x * (GELU_C1 + GELU_C2 * (x * x)))


def _const_spec(shape):
    nd = len(shape)
    return pl.BlockSpec(shape, lambda *_: (0,) * nd, pipeline_mode=pl.Buffered(1))


def _inproj_kernel(decb_ref, x_ref, gpre_ref, w_ref, cos_ref, sin_ref, kcos_ref, ksin_ref,
                   lng_ref, lnb_ref, zb_ref, wup_f32_ref, wdn_f32_ref,
                   q_ref, k_ref, v_ref, gs_ref, u_ref, sv_ref, sr_ref, ss_ref, rb_ref,
                   wup_ref, wdn_ref,
                   kz_scr, state_ref):
    @pl.when(pl.program_id(1) == 0)
    def _():
        state_ref[...] = jnp.zeros_like(state_ref)

    wup_ref[...] = wup_f32_ref[...].astype(BF16)
    wdn_ref[...] = wdn_f32_ref[...].astype(BF16)

    h = _rms(x_ref[0], gpre_ref[...]).astype(BF16)

    def proj(idx):
        off = IN_OFFS[idx]
        return _dot(h, w_ref[:, off:off + IN_SPLITS[idx]])

    def rotary(t, c_ref, s_ref):
        outs = []
        for hh in range(RET_HEADS):
            th = t[:, hh * RET_DK:(hh + 1) * RET_DK]
            outs.append(th * c_ref[...] + pltpu.roll(th, RET_DK // 2, axis=1) * s_ref[...])
        return outs

    sv = _gelu(proj(5))
    mu = jnp.mean(sv, axis=-1, keepdims=True)
    svc = sv - mu
    var = jnp.mean(svc * svc, axis=-1, keepdims=True)
    sv_ref[0] = (svc * lax.rsqrt(var + EPS) * lng_ref[...] + lnb_ref[...]).astype(BF16)

    ks = rotary(proj(1), kcos_ref, ksin_ref)
    for hh in range(RET_HEADS):
        k_ref[0, :, hh * RET_DK:(hh + 1) * RET_DK] = ks[hh].astype(BF16)
        for c in range(CHUNKS_PER_TILE):
            rows = slice(c * CHUNK, (c + 1) * CHUNK)
            kz_scr[hh, rows, :] = ks[hh][rows, :] * zb_ref[hh]
    v_ref[0] = proj(2).astype(BF16)

    for c in reversed(range(CHUNKS_PER_TILE)):
        rows = slice(c * CHUNK, (c + 1) * CHUNK)
        for hh in range(RET_HEADS):
            rb_ref[0, c, hh] = state_ref[hh].astype(BF16)
            kzt = kz_scr[hh, rows, :].T.astype(BF16)
            vh = v_ref[0, rows, hh * RET_DV:(hh + 1) * RET_DV]
            state_ref[hh] = decb_ref[hh] * state_ref[hh] + _dot(kzt, vh)

    qs = rotary(proj(0), cos_ref, sin_ref)
    for hh in range(RET_HEADS):
        q_ref[0, :, hh * RET_DK:(hh + 1) * RET_DK] = qs[hh].astype(BF16)
    hg = 0.5 * proj(3)
    gs_ref[0] = (hg + hg * jnp.tanh(hg)).astype(BF16)
    u_ref[0] = _gelu(proj(4)).astype(BF16)
    sr_ref[0] = _sigmoid(proj(6)).astype(BF16)
    ss_ref[0] = _sigmoid(proj(7)).astype(BF16)


def _cast_block_spec(rows, cols, steps, nT):
    block = next(r for r in range(CAST_ROW_ALIGN, rows + 1, CAST_ROW_ALIGN)
                 if rows % r == 0 and rows // r <= steps)
    n_blocks = rows // block
    return pl.BlockSpec((block, cols), lambda b, j: (jnp.minimum(b * nT + j, n_blocks - 1), 0))


def _inproj_call(x, g_pre, w_in, cos_t, sin_t, kcos_t, ksin_t, ln_g, ln_b, zb, decb, w_up, w_down):
    B, S, _ = x.shape
    TM = TOKEN_TILE
    nT = S // TM
    N = S // CHUNK
    rev = lambda b, j: (b, nT - 1 - j, 0)
    rev_tab = lambda b, j: (nT - 1 - j, 0)

    def tok(width):
        return pl.BlockSpec((1, TM, width), rev)

    wup_spec = _cast_block_spec(*w_up.shape, B * nT, nT)
    wdn_spec = _cast_block_spec(*w_down.shape, B * nT, nT)
    out_shapes = [jax.ShapeDtypeStruct((B, S, w), BF16) for w in IN_SPLITS]
    out_shapes.append(jax.ShapeDtypeStruct((B, N, RET_HEADS, RET_DK, RET_DV), BF16))
    out_shapes += [jax.ShapeDtypeStruct(w_up.shape, BF16), jax.ShapeDtypeStruct(w_down.shape, BF16)]
    out_specs = [tok(w) for w in IN_SPLITS]
    out_specs.append(pl.BlockSpec((1, CHUNKS_PER_TILE, RET_HEADS, RET_DK, RET_DV),
                                  lambda b, j: (b, nT - 1 - j, 0, 0, 0)))
    out_specs += [wup_spec, wdn_spec]
    tab = pl.BlockSpec((TM, RET_DK), rev_tab)
    return pl.pallas_call(
        _inproj_kernel,
        grid=(B, nT),
        in_specs=[
            pl.BlockSpec(memory_space=pltpu.SMEM),
            tok(D_MODEL),
            _const_spec((1, D_MODEL)),
            _const_spec((D_MODEL, IN_W)),
            tab, tab, tab, tab,
            _const_spec((1, SGU_W)),
            _const_spec((1, SGU_W)),
            _const_spec((RET_HEADS, CHUNK, RET_DK)),
            wup_spec, wdn_spec,
        ],
        out_specs=out_specs,
        out_shape=out_shapes,
        scratch_shapes=[
            pltpu.VMEM((RET_HEADS, TM, RET_DK), F32),
            pltpu.VMEM((RET_HEADS, RET_DK, RET_DV), F32),
        ],
        compiler_params=pltpu.CompilerParams(
            dimension_semantics=("arbitrary", "arbitrary"),
            vmem_limit_bytes=VMEM_LIMIT_BYTES),
        name="inproj",
    )(decb, x, g_pre, w_in, cos_t, sin_t, kcos_t, ksin_t, ln_g, ln_b, zb, w_up, w_down)


def _mixer_kernel(decf_ref, x_ref, q_ref, k_ref, v_ref, gs_ref, u_ref, sv_ref, sr_ref, ss_ref,
                  rb_ref, dmat_ref, xif_ref, xib_ref, zf_ref, wsp_ref, bsp_ref,
                  wro_ref, wso_ref, wout_ref, gpm_ref, gpf_ref,
                  x1_ref, h2_ref,
                  o_scr, y_scr, rf_scr):
    @pl.when(pl.program_id(1) == 0)
    def _():
        rf_scr[...] = jnp.zeros_like(rf_scr)

    heads = range(RET_HEADS)
    kcs = [slice(hh * RET_DK, (hh + 1) * RET_DK) for hh in heads]
    vcs = [slice(hh * RET_DV, (hh + 1) * RET_DV) for hh in heads]

    def token_mix(c):
        rows = slice(c * CHUNK, (c + 1) * CHUNK)
        qhs = [q_ref[0, rows, kcs[hh]] for hh in heads]
        khs = [k_ref[0, rows, kcs[hh]] for hh in heads]
        vhs = [v_ref[0, rows, vcs[hh]] for hh in heads]
        ss = [lax.dot_general(qhs[hh], khs[hh], (((1,), (1,)), ((), ())),
                              preferred_element_type=F32) for hh in heads]
        rfs = [rf_scr[hh] for hh in heads]
        kvs = [_dot((khs[hh].astype(F32) * zf_ref[hh]).T.astype(BF16), vhs[hh]) for hh in heads]
        outs = []
        for hh in heads:
            p = (ss[hh] * dmat_ref[hh]).astype(BF16)
            qf = qhs[hh].astype(F32)
            qxf = (qf * xif_ref[hh]).astype(BF16)
            qxb = (qf * xib_ref[hh]).astype(BF16)
            lhs = jnp.concatenate([p, qxf], axis=1)
            rhs = jnp.concatenate([vhs[hh], rfs[hh].astype(BF16)], axis=0)
            outs.append(_dot(lhs, rhs) + _dot(qxb, rb_ref[0, c, hh]))
        for hh in heads:
            rf_scr[hh] = decf_ref[hh] * rfs[hh] + kvs[hh]
        for hh in heads:
            o = outs[hh]
            mu = jnp.mean(o, axis=-1, keepdims=True)
            oc = o - mu
            var = jnp.mean(oc * oc, axis=-1, keepdims=True)
            on = oc * lax.rsqrt(var + EPS)
            o_scr[rows, vcs[hh]] = (on * gs_ref[0, rows, vcs[hh]].astype(F32)).astype(BF16)
        for g in range(SGU_GROUPS):
            gc = slice(g * SGU_GW, (g + 1) * SGU_GW)
            mixed = _dot(wsp_ref[g], sv_ref[0, rows, gc]) + bsp_ref[g]
            y_scr[rows, gc] = (u_ref[0, rows, gc].astype(F32) * mixed).astype(BF16)

    def dense(r0):
        rows = slice(r0, r0 + DENSE_ROWS)
        y_ret = _dot(o_scr[rows, :], wro_ref[...])
        y_sgu = _dot(y_scr[rows, :], wso_ref[...])
        merged = (sr_ref[0, rows, :].astype(F32) * y_ret
                  + ss_ref[0, rows, :].astype(F32) * y_sgu).astype(BF16)
        x1 = x_ref[0, rows, :] + _rms(_dot(merged, wout_ref[...]), gpm_ref[...])
        x1_ref[0, rows, :] = x1
        h2_ref[0, rows, :] = _rms(x1, gpf_ref[...]).astype(BF16)

    for c in range(CHUNKS_PER_TILE):
        token_mix(c)
    for r0 in range(0, TOKEN_TILE, DENSE_ROWS):
        dense(r0)


def _mixer_call(x, q, k, v, gs, u, sv, sr, ss, rb, decf, dmat, xif, xib, zf, wsp, bsp,
                w_ret_o, w_sgu_o, w_out, g_post_mix, g_pre_ffn):
    B, S, _ = x.shape
    TM = TOKEN_TILE
    nT = S // TM
    fwd = lambda b, j: (b, j, 0)

    def tok(width):
        return pl.BlockSpec((1, TM, width), fwd)

    head_tab = _const_spec((RET_HEADS, CHUNK, RET_DK))
    return pl.pallas_call(
        _mixer_kernel,
        grid=(B, nT),
        in_specs=[
            pl.BlockSpec(memory_space=pltpu.SMEM),
            tok(D_MODEL),
            tok(RET_QK_W), tok(RET_QK_W), tok(RET_V_W), tok(RET_V_W),
            tok(SGU_W), tok(SGU_W), tok(D_MODEL), tok(D_MODEL),
            pl.BlockSpec((1, CHUNKS_PER_TILE, RET_HEADS, RET_DK, RET_DV),
                         lambda b, j: (b, j, 0, 0, 0)),
            head_tab, head_tab, head_tab, head_tab,
            _const_spec((SGU_GROUPS, CHUNK, CHUNK)),
            _const_spec((SGU_GROUPS, CHUNK, SGU_GW)),
            _const_spec((RET_V_W, D_MODEL)),
            _const_spec((SGU_W, D_MODEL)),
            _const_spec((D_MODEL, D_MODEL)),
            _const_spec((1, D_MODEL)),
            _const_spec((1, D_MODEL)),
        ],
        out_specs=[tok(D_MODEL), tok(D_MODEL)],
        out_shape=[jax.ShapeDtypeStruct((B, S, D_MODEL), F32),
                   jax.ShapeDtypeStruct((B, S, D_MODEL), BF16)],
        scratch_shapes=[
            pltpu.VMEM((TM, RET_V_W), BF16),
            pltpu.VMEM((TM, SGU_W), BF16),
            pltpu.VMEM((RET_HEADS, RET_DK, RET_DV), F32),
        ],
        compiler_params=pltpu.CompilerParams(
            dimension_semantics=("arbitrary", "arbitrary"),
            vmem_limit_bytes=VMEM_LIMIT_BYTES),
        name="mixer",
    )(decf, x, q, k, v, gs, u, sv, sr, ss, rb, dmat, xif, xib, zf, wsp, bsp,
      w_ret_o, w_sgu_o, w_out, g_post_mix, g_pre_ffn)


def _ffn_kernel(x1_ref, h2_ref, prev_ref, next_ref, wup_ref, cw_ref, cb_ref, wdn_ref, gpost_ref,
                out_ref, lhs_scr, act_scr):
    TM = FFN_TILE
    rows_all = TM + 2 * HALO
    j = pl.program_id(1)
    last = pl.num_programs(1) - 1
    lhs_scr[0:HALO, :] = jnp.where(j == 0, jnp.zeros_like(prev_ref[0]), prev_ref[0])
    lhs_scr[HALO:HALO + TM, :] = h2_ref[0]
    lhs_scr[HALO + TM:rows_all, :] = jnp.where(j == last, jnp.zeros_like(next_ref[0]), next_ref[0])
    lhs = lhs_scr[...]

    def conv(hid, col):
        w = cw_ref[:, col:col + FF_BLOCK]
        b = cb_ref[:, col:col + FF_BLOCK]
        dn = pltpu.roll(hid, 1, axis=0)[HALO:HALO + TM]
        up = pltpu.roll(hid, rows_all - 1, axis=0)[HALO:HALO + TM]
        return w[0:1] * dn + w[1:2] * hid[HALO:HALO + TM] + w[2:3] * up + b

    for fb in range(D_FF // FF_BLOCK):
        ca = conv(_dot(lhs, wup_ref[:, fb * FF_BLOCK:(fb + 1) * FF_BLOCK]), fb * FF_BLOCK)
        cbh = conv(_dot(lhs, wup_ref[:, D_FF + fb * FF_BLOCK:D_FF + (fb + 1) * FF_BLOCK]),
                   D_FF + fb * FF_BLOCK)
        inner = ca * (GELU_C1 + GELU_C2 * (ca * ca))
        act = (ca * cbh) * (1.0 + jnp.tanh(inner))
        act_scr[:, fb * FF_BLOCK:(fb + 1) * FF_BLOCK] = act.astype(BF16)

    out_ref[0] = x1_ref[0] + _rms(_dot(act_scr[...], wdn_ref[...]), gpost_ref[...])


def _ffn_call(x1, h2, w_up, conv_w, conv_b, w_down, g_post_ffn):
    B, S, _ = x1.shape
    TM = FFN_TILE
    nT = S // TM
    hpt = TM // HALO
    n_halo = S // HALO
    tok = lambda w: pl.BlockSpec((1, TM, w), lambda b, j: (b, j, 0))
    return pl.pallas_call(
        _ffn_kernel,
        grid=(B, nT),
        in_specs=[
            tok(D_MODEL),
            tok(D_MODEL),
            pl.BlockSpec((1, HALO, D_MODEL), lambda b, j: (b, jnp.maximum(j * hpt - 1, 0), 0)),
            pl.BlockSpec((1, HALO, D_MODEL),
                         lambda b, j: (b, jnp.minimum((j + 1) * hpt, n_halo - 1), 0)),
            _const_spec((D_MODEL, 2 * D_FF)),
            _const_spec((CONV_W, 2 * D_FF)),
            _const_spec((1, 2 * D_FF)),
            _const_spec((D_FF, D_MODEL)),
            _const_spec((1, D_MODEL)),
        ],
        out_specs=tok(D_MODEL),
        out_shape=jax.ShapeDtypeStruct((B, S, D_MODEL), F32),
        scratch_shapes=[
            pltpu.VMEM((TM + 2 * HALO, D_MODEL), BF16),
            pltpu.VMEM((TM, D_FF), BF16),
        ],
        compiler_params=pltpu.CompilerParams(
            dimension_semantics=("arbitrary", "arbitrary"),
            vmem_limit_bytes=VMEM_LIMIT_BYTES),
        name="ffn",
    )(x1, h2, h2, h2, w_up, conv_w, conv_b, w_down, g_post_ffn)


def _rotary_tables(S):
    half = RET_DK // 2
    f32 = np.float32
    inv_freq = (f32(1.0) / (f32(ROPE_BASE) ** (np.arange(half, dtype=f32) / f32(half)))).astype(f32)
    ang = np.arange(S, dtype=f32)[:, None] * inv_freq[None, :]
    cos = np.cos(ang.astype(np.float64))
    sin = np.sin(ang.astype(np.float64))
    cos_t = np.concatenate([cos, cos], axis=-1)
    sin_t = np.concatenate([-sin, sin], axis=-1)
    scale = RET_DK ** -0.5
    return tuple(jnp.asarray(t.astype(f32)) for t in (cos_t, sin_t, cos_t * scale, sin_t * scale))


def _decay_tables(ret_decay_logit):
    lg = jax.nn.log_sigmoid(ret_decay_logit.astype(F32))
    lgf, lgb = lg[0][:, None, None], lg[1][:, None, None]
    idx = jnp.arange(CHUNK, dtype=F32)
    diff = idx[:, None] - idx[None, :]
    dmat = jnp.where(diff >= 0, jnp.exp(jnp.maximum(diff, 0.0)[None] * lgf),
                     jnp.exp(jnp.maximum(-diff, 0.0)[None] * lgb))
    col = jnp.broadcast_to(idx[None, :, None], (RET_HEADS, CHUNK, RET_DK))
    xif = jnp.exp((col + 1.0) * lgf)
    xib = jnp.exp((CHUNK - col) * lgb)
    zf = jnp.exp((CHUNK - 1.0 - col) * lgf)
    zb = jnp.exp(col * lgb)
    decf = jnp.exp(CHUNK * lg[0])
    decb = jnp.exp(CHUNK * lg[1])
    return dmat, xif, xib, zf, zb, decf, decb


def kernel(x, g_pre_mix, w_in, ret_decay_logit, sgu_ln_g, sgu_ln_b, w_spatial, b_spatial,
           w_ret_o, w_sgu_o, w_out, g_post_mix, g_pre_ffn, w_up, conv_w, conv_b, w_down,
           g_post_ffn):
    B, S, D = x.shape
    assert D == D_MODEL and S % TOKEN_TILE == 0 and S % FFN_TILE == 0
    row = lambda a: a.astype(F32).reshape(1, -1)
    cos_t, sin_t, kcos_t, ksin_t = _rotary_tables(S)
    dmat, xif, xib, zf, zb, decf, decb = _decay_tables(ret_decay_logit)
    bsp = jnp.broadcast_to(b_spatial.astype(F32)[:, :, None], (SGU_GROUPS, CHUNK, SGU_GW))

    q, k, v, gs, u, sv, sr, ss, rb, w_up_bf16, w_down_bf16 = _inproj_call(
        x, row(g_pre_mix), w_in.astype(BF16), cos_t, sin_t, kcos_t, ksin_t,
        row(sgu_ln_g), row(sgu_ln_b), zb, decb, w_up.astype(F32), w_down.astype(F32))
    x1, h2 = _mixer_call(
        x, q, k, v, gs, u, sv, sr, ss, rb, decf, dmat, xif, xib, zf,
        w_spatial.astype(BF16), bsp, w_ret_o.astype(BF16), w_sgu_o.astype(BF16),
        w_out.astype(BF16), row(g_post_mix), row(g_pre_ffn))
    gate_half = jnp.concatenate([jnp.ones((D_FF,), F32), jnp.full((D_FF,), 0.5, F32)])[None, :]
    return _ffn_call(x1, h2, w_up_bf16, conv_w.astype(F32) * gate_half,
                     row(conv_b) * gate_half, w_down_bf16, row(g_post_ffn))
```

```python
import math

import jax
import jax.numpy as jnp
import numpy as np
from jax import lax
from jax.experimental import pallas as pl
from jax.experimental.pallas import tpu as pltpu

D_MODEL = 1024
CHUNK = 128
RET_HEADS = 4
RET_DK = 128
RET_DV = 256
RET_QK_W = RET_HEADS * RET_DK
RET_V_W = RET_HEADS * RET_DV
SGU_GROUPS = 4
SGU_W = 1024
SGU_GW = SGU_W // SGU_GROUPS
D_FF = 2816
CONV_W = 3
ROPE_BASE = 10000.0
EPS = 1e-6
IN_SPLITS = (RET_QK_W, RET_QK_W, RET_V_W, RET_V_W, SGU_W, SGU_W, D_MODEL, D_MODEL)
IN_W = sum(IN_SPLITS)
IN_OFFS = tuple(sum(IN_SPLITS[:i]) for i in range(len(IN_SPLITS)))

TOKEN_TILE = 512
CHUNKS_PER_TILE = TOKEN_TILE // CHUNK
FFN_TILE = 1024
FFN_DOWN_ROWS = 512
LANES = 128
HID_SLABS = 4
CAST_ROW_ALIGN = 16
DENSE_ROWS = 256
HALO = 16
FF_BLOCK = 256
GELU_C1 = math.sqrt(2.0 / math.pi)
GELU_C2 = GELU_C1 * 0.044715
VMEM_LIMIT_BYTES = 56 * 1024 * 1024

F32 = jnp.float32
BF16 = jnp.bfloat16


def _dot(a, b):
    return jnp.dot(a, b, preferred_element_type=F32)


def _rms(x, g):
    return x * lax.rsqrt(jnp.mean(x * x, axis=-1, keepdims=True) + EPS) * g


def _sigmoid(x):
    return 0.5 * jnp.tanh(0.5 * x) + 0.5


def _gelu(x):
    hx = 0.5 * x
    return hx + hx * jnp.tanh(x * (GELU_C1 + GELU_C2 * (x * x)))


def _const_spec(shape):
    nd = len(shape)
    return pl.BlockSpec(shape, lambda *_: (0,) * nd, pipeline_mode=pl.Buffered(1))


def _inproj_kernel(decb_ref, x_ref, gpre_ref, w_ref, cos_ref, sin_ref, kcos_ref, ksin_ref,
                   lng_ref, lnb_ref, zb_ref, wup_f32_ref, wdn_f32_ref,
                   q_ref, k_ref, v_ref, gs_ref, u_ref, sv_ref, sr_ref, ss_ref, rb_ref,
                   wup_ref, wdn_ref,
                   kz_scr, state_ref):
    @pl.when(pl.program_id(1) == 0)
    def _():
        state_ref[...] = jnp.zeros_like(state_ref)

    wup_ref[...] = wup_f32_ref[...].astype(BF16)
    wdn_ref[...] = wdn_f32_ref[...].astype(BF16)

    h = _rms(x_ref[0], gpre_ref[...]).astype(BF16)

    def proj(idx):
        off = IN_OFFS[idx]
        return _dot(h, w_ref[:, off:off + IN_SPLITS[idx]])

    def rotary(t, c_ref, s_ref):
        outs = []
        for hh in range(RET_HEADS):
            th = t[:, hh * RET_DK:(hh + 1) * RET_DK]
            outs.append(th * c_ref[...] + pltpu.roll(th, RET_DK // 2, axis=1) * s_ref[...])
        return outs

    sv = _gelu(proj(5))
    mu = jnp.mean(sv, axis=-1, keepdims=True)
    svc = sv - mu
    var = jnp.mean(svc * svc, axis=-1, keepdims=True)
    sv_ref[0] = (svc * lax.rsqrt(var + EPS) * lng_ref[...] + lnb_ref[...]).astype(BF16)

    ks = rotary(proj(1), kcos_ref, ksin_ref)
    for hh in range(RET_HEADS):
        k_ref[0, :, hh * RET_DK:(hh + 1) * RET_DK] = ks[hh].astype(BF16)
        for c in range(CHUNKS_PER_TILE):
            rows = slice(c * CHUNK, (c + 1) * CHUNK)
            kz_scr[hh, rows, :] = ks[hh][rows, :] * zb_ref[hh]
    v_ref[0] = proj(2).astype(BF16)

    for c in reversed(range(CHUNKS_PER_TILE)):
        rows = slice(c * CHUNK, (c + 1) * CHUNK)
        for hh in range(RET_HEADS):
            rb_ref[0, c, hh] = state_ref[hh].astype(BF16)
            kzt = kz_scr[hh, rows, :].T.astype(BF16)
            vh = v_ref[0, rows, hh * RET_DV:(hh + 1) * RET_DV]
            state_ref[hh] = decb_ref[hh] * state_ref[hh] + _dot(kzt, vh)

    qs = rotary(proj(0), cos_ref, sin_ref)
    for hh in range(RET_HEADS):
        q_ref[0, :, hh * RET_DK:(hh + 1) * RET_DK] = qs[hh].astype(BF16)
    hg = 0.5 * proj(3)
    gs_ref[0] = (hg + hg * jnp.tanh(hg)).astype(BF16)
    u_ref[0] = _gelu(proj(4)).astype(BF16)
    sr_ref[0] = _sigmoid(proj(6)).astype(BF16)
    ss_ref[0] = _sigmoid(proj(7)).astype(BF16)


def _cast_block_spec(rows, cols, steps, nT):
    block = next(r for r in range(CAST_ROW_ALIGN, rows + 1, CAST_ROW_ALIGN)
                 if rows % r == 0 and rows // r <= steps)
    n_blocks = rows // block
    return pl.BlockSpec((block, cols), lambda b, j: (jnp.minimum(b * nT + j, n_blocks - 1), 0))


def _inproj_call(x, g_pre, w_in, cos_t, sin_t, kcos_t, ksin_t, ln_g, ln_b, zb, decb, w_up, w_down):
    B, S, _ = x.shape
    TM = TOKEN_TILE
    nT = S // TM
    N = S // CHUNK
    rev = lambda b, j: (b, nT - 1 - j, 0)
    rev_tab = lambda b, j: (nT - 1 - j, 0)

    def tok(width):
        return pl.BlockSpec((1, TM, width), rev)

    wup_spec = _cast_block_spec(*w_up.shape, B * nT, nT)
    wdn_spec = _cast_block_spec(*w_down.shape, B * nT, nT)
    out_shapes = [jax.ShapeDtypeStruct((B, S, w), BF16) for w in IN_SPLITS]
    out_shapes.append(jax.ShapeDtypeStruct((B, N, RET_HEADS, RET_DK, RET_DV), BF16))
    out_shapes += [jax.ShapeDtypeStruct(w_up.shape, BF16), jax.ShapeDtypeStruct(w_down.shape, BF16)]
    out_specs = [tok(w) for w in IN_SPLITS]
    out_specs.append(pl.BlockSpec((1, CHUNKS_PER_TILE, RET_HEADS, RET_DK, RET_DV),
                                  lambda b, j: (b, nT - 1 - j, 0, 0, 0)))
    out_specs += [wup_spec, wdn_spec]
    tab = pl.BlockSpec((TM, RET_DK), rev_tab)
    return pl.pallas_call(
        _inproj_kernel,
        grid=(B, nT),
        in_specs=[
            pl.BlockSpec(memory_space=pltpu.SMEM),
            tok(D_MODEL),
            _const_spec((1, D_MODEL)),
            _const_spec((D_MODEL, IN_W)),
            tab, tab, tab, tab,
            _const_spec((1, SGU_W)),
            _const_spec((1, SGU_W)),
            _const_spec((RET_HEADS, CHUNK, RET_DK)),
            wup_spec, wdn_spec,
        ],
        out_specs=out_specs,
        out_shape=out_shapes,
        scratch_shapes=[
            pltpu.VMEM((RET_HEADS, TM, RET_DK), F32),
            pltpu.VMEM((RET_HEADS, RET_DK, RET_DV), F32),
        ],
        compiler_params=pltpu.CompilerParams(
            dimension_semantics=("arbitrary", "arbitrary"),
            vmem_limit_bytes=VMEM_LIMIT_BYTES),
        name="inproj",
    )(decb, x, g_pre, w_in, cos_t, sin_t, kcos_t, ksin_t, ln_g, ln_b, zb, w_up, w_down)


def _mixer_kernel(decf_ref, x_ref, q_ref, k_ref, v_ref, gs_ref, u_ref, sv_ref, sr_ref, ss_ref,
                  rb_ref, dmat_ref, xif_ref, xib_ref, zf_ref, wsp_ref, bsp_ref,
                  wro_ref, wso_ref, wout_ref, gpm_ref, gpf_ref,
                  x1_ref, h2_ref,
                  o_scr, y_scr, rf_scr):
    @pl.when(pl.program_id(1) == 0)
    def _():
        rf_scr[...] = jnp.zeros_like(rf_scr)

    heads = range(RET_HEADS)
    kcs = [slice(hh * RET_DK, (hh + 1) * RET_DK) for hh in heads]
    vcs = [slice(hh * RET_DV, (hh + 1) * RET_DV) for hh in heads]

    def token_mix(c):
        rows = slice(c * CHUNK, (c + 1) * CHUNK)
        qhs = [q_ref[0, rows, kcs[hh]] for hh in heads]
        khs = [k_ref[0, rows, kcs[hh]] for hh in heads]
        vhs = [v_ref[0, rows, vcs[hh]] for hh in heads]
        ss = [lax.dot_general(qhs[hh], khs[hh], (((1,), (1,)), ((), ())),
                              preferred_element_type=F32) for hh in heads]
        rfs = [rf_scr[hh] for hh in heads]
        kvs = [_dot((khs[hh].astype(F32) * zf_ref[hh]).T.astype(BF16), vhs[hh]) for hh in heads]
        outs = []
        for hh in heads:
            p = (ss[hh] * dmat_ref[hh]).astype(BF16)
            qf = qhs[hh].astype(F32)
            qxf = (qf * xif_ref[hh]).astype(BF16)
            qxb = (qf * xib_ref[hh]).astype(BF16)
            lhs = jnp.concatenate([p, qxf], axis=1)
            rhs = jnp.concatenate([vhs[hh], rfs[hh].astype(BF16)], axis=0)
            outs.append(_dot(lhs, rhs) + _dot(qxb, rb_ref[0, c, hh]))
        for hh in heads:
            rf_scr[hh] = decf_ref[hh] * rfs[hh] + kvs[hh]
        for hh in heads:
            o = outs[hh]
            mu = jnp.mean(o, axis=-1, keepdims=True)
            oc = o - mu
            var = jnp.mean(oc * oc, axis=-1, keepdims=True)
            on = oc * lax.rsqrt(var + EPS)
            o_scr[rows, vcs[hh]] = (on * gs_ref[0, rows, vcs[hh]].astype(F32)).astype(BF16)
        for g in range(SGU_GROUPS):
            gc = slice(g * SGU_GW, (g + 1) * SGU_GW)
            mixed = _dot(wsp_ref[g], sv_ref[0, rows, gc]) + bsp_ref[g]
            y_scr[rows, gc] = (u_ref[0, rows, gc].astype(F32) * mixed).astype(BF16)

    def dense(r0):
        rows = slice(r0, r0 + DENSE_ROWS)
        y_ret = _dot(o_scr[rows, :], wro_ref[...])
        y_sgu = _dot(y_scr[rows, :], wso_ref[...])
        merged = (sr_ref[0, rows, :].astype(F32) * y_ret
                  + ss_ref[0, rows, :].astype(F32) * y_sgu).astype(BF16)
        x1 = x_ref[0, rows, :] + _rms(_dot(merged, wout_ref[...]), gpm_ref[...])
        x1_ref[0, rows, :] = x1
        h2_ref[0, rows, :] = _rms(x1, gpf_ref[...]).astype(BF16)

    for c in range(CHUNKS_PER_TILE):
        token_mix(c)
    for r0 in range(0, TOKEN_TILE, DENSE_ROWS):
        dense(r0)


def _mixer_call(x, q, k, v, gs, u, sv, sr, ss, rb, decf, dmat, xif, xib, zf, wsp, bsp,
                w_ret_o, w_sgu_o, w_out, g_post_mix, g_pre_ffn):
    B, S, _ = x.shape
    TM = TOKEN_TILE
    nT = S // TM
    fwd = lambda b, j: (b, j, 0)

    def tok(width):
        return pl.BlockSpec((1, TM, width), fwd)

    head_tab = _const_spec((RET_HEADS, CHUNK, RET_DK))
    return pl.pallas_call(
        _mixer_kernel,
        grid=(B, nT),
        in_specs=[
            pl.BlockSpec(memory_space=pltpu.SMEM),
            tok(D_MODEL),
            tok(RET_QK_W), tok(RET_QK_W), tok(RET_V_W), tok(RET_V_W),
            tok(SGU_W), tok(SGU_W), tok(D_MODEL), tok(D_MODEL),
            pl.BlockSpec((1, CHUNKS_PER_TILE, RET_HEADS, RET_DK, RET_DV),
                         lambda b, j: (b, j, 0, 0, 0)),
            head_tab, head_tab, head_tab, head_tab,
            _const_spec((SGU_GROUPS, CHUNK, CHUNK)),
            _const_spec((SGU_GROUPS, CHUNK, SGU_GW)),
            _const_spec((RET_V_W, D_MODEL)),
            _const_spec((SGU_W, D_MODEL)),
            _const_spec((D_MODEL, D_MODEL)),
            _const_spec((1, D_MODEL)),
            _const_spec((1, D_MODEL)),
        ],
        out_specs=[tok(D_MODEL), tok(D_MODEL)],
        out_shape=[jax.ShapeDtypeStruct((B, S, D_MODEL), F32),
                   jax.ShapeDtypeStruct((B, S, D_MODEL), BF16)],
        scratch_shapes=[
            pltpu.VMEM((TM, RET_V_W), BF16),
            pltpu.VMEM((TM, SGU_W), BF16),
            pltpu.VMEM((RET_HEADS, RET_DK, RET_DV), F32),
        ],
        compiler_params=pltpu.CompilerParams(
            dimension_semantics=("arbitrary", "arbitrary"),
            vmem_limit_bytes=VMEM_LIMIT_BYTES),
        name="mixer",
    )(decf, x, q, k, v, gs, u, sv, sr, ss, rb, dmat, xif, xib, zf, wsp, bsp,
      w_ret_o, w_sgu_o, w_out, g_post_mix, g_pre_ffn)


def _ffn_kernel(x1_ref, h2_ref, prev_ref, next_ref, wup_ref, cw_ref, cb_ref, wdn_ref, gpost_ref,
                out_ref, lhs_scr, act_scr, hid_scr):
    TM = FFN_TILE
    rows_all = TM + 2 * HALO
    slab_uses = [0]
    j = pl.program_id(1)
    last = pl.num_programs(1) - 1
    lhs_scr[0:HALO, :] = jnp.where(j == 0, jnp.zeros_like(prev_ref[0]), prev_ref[0])
    lhs_scr[HALO:HALO + TM, :] = h2_ref[0]
    lhs_scr[HALO + TM:rows_all, :] = jnp.where(j == last, jnp.zeros_like(next_ref[0]), next_ref[0])
    lhs = lhs_scr[...]

    def conv(hid, col):
        k = slab_uses[0] % HID_SLABS
        slab_uses[0] += 1
        for t in range(FF_BLOCK // LANES):
            hid_scr[k, t] = hid[:, t * LANES:(t + 1) * LANES]
        outs = []
        for t in range(FF_BLOCK // LANES):
            c0 = col + t * LANES
            w = cw_ref[:, c0:c0 + LANES]
            b = cb_ref[:, c0:c0 + LANES]
            dn = hid_scr[k, t, pl.ds(HALO - 1, TM), :]
            mid = hid[HALO:HALO + TM, t * LANES:(t + 1) * LANES]
            up = hid_scr[k, t, pl.ds(HALO + 1, TM), :]
            outs.append(w[0:1] * dn + w[1:2] * mid + w[2:3] * up + b)
        return jnp.concatenate(outs, axis=1)

    for fb in range(D_FF // FF_BLOCK):
        ca = conv(_dot(lhs, wup_ref[:, fb * FF_BLOCK:(fb + 1) * FF_BLOCK]), fb * FF_BLOCK)
        cbh = conv(_dot(lhs, wup_ref[:, D_FF + fb * FF_BLOCK:D_FF + (fb + 1) * FF_BLOCK]),
                   D_FF + fb * FF_BLOCK)
        inner = ca * (GELU_C1 + GELU_C2 * (ca * ca))
        act = (ca * cbh) * (1.0 + jnp.tanh(inner))
        act_scr[:, fb * FF_BLOCK:(fb + 1) * FF_BLOCK] = act.astype(BF16)

    for r0 in range(0, TM, FFN_DOWN_ROWS):
        rows = slice(r0, r0 + FFN_DOWN_ROWS)
        out_ref[0, rows, :] = x1_ref[0, rows, :] + _rms(_dot(act_scr[rows, :], wdn_ref[...]),
                                                        gpost_ref[...])


def _ffn_call(x1, h2, w_up, conv_w, conv_b, w_down, g_post_ffn):
    B, S, _ = x1.shape
    TM = FFN_TILE
    nT = S // TM
    hpt = TM // HALO
    n_halo = S // HALO
    tok = lambda w: pl.BlockSpec((1, TM, w), lambda b, j: (b, j, 0))
    return pl.pallas_call(
        _ffn_kernel,
        grid=(B, nT),
        in_specs=[
            tok(D_MODEL),
            tok(D_MODEL),
            pl.BlockSpec((1, HALO, D_MODEL), lambda b, j: (b, jnp.maximum(j * hpt - 1, 0), 0)),
            pl.BlockSpec((1, HALO, D_MODEL),
                         lambda b, j: (b, jnp.minimum((j + 1) * hpt, n_halo - 1), 0)),
            _const_spec((D_MODEL, 2 * D_FF)),
            _const_spec((CONV_W, 2 * D_FF)),
            _const_spec((1, 2 * D_FF)),
            _const_spec((D_FF, D_MODEL)),
            _const_spec((1, D_MODEL)),
        ],
        out_specs=tok(D_MODEL),
        out_shape=jax.ShapeDtypeStruct((B, S, D_MODEL), F32),
        scratch_shapes=[
            pltpu.VMEM((TM + 2 * HALO, D_MODEL), BF16),
            pltpu.VMEM((TM, D_FF), BF16),
            pltpu.VMEM((HID_SLABS, FF_BLOCK // LANES, TM + 2 * HALO, LANES), F32),
        ],
        compiler_params=pltpu.CompilerParams(
            dimension_semantics=("arbitrary", "arbitrary"),
            vmem_limit_bytes=VMEM_LIMIT_BYTES),
        name="ffn",
    )(x1, h2, h2, h2, w_up, conv_w, conv_b, w_down, g_post_ffn)


def _rotary_tables(S):
    half = RET_DK // 2
    f32 = np.float32
    inv_freq = (f32(1.0) / (f32(ROPE_BASE) ** (np.arange(half, dtype=f32) / f32(half)))).astype(f32)
    ang = np.arange(S, dtype=f32)[:, None] * inv_freq[None, :]
    cos = np.cos(ang.astype(np.float64))
    sin = np.sin(ang.astype(np.float64))
    cos_t = np.concatenate([cos, cos], axis=-1)
    sin_t = np.concatenate([-sin, sin], axis=-1)
    scale = RET_DK ** -0.5
    return tuple(jnp.asarray(t.astype(f32)) for t in (cos_t, sin_t, cos_t * scale, sin_t * scale))


def _decay_tables(ret_decay_logit):
    lg = jax.nn.log_sigmoid(ret_decay_logit.astype(F32))
    lgf, lgb = lg[0][:, None, None], lg[1][:, None, None]
    idx = jnp.arange(CHUNK, dtype=F32)
    diff = idx[:, None] - idx[None, :]
    dmat = jnp.where(diff >= 0, jnp.exp(jnp.maximum(diff, 0.0)[None] * lgf),
                     jnp.exp(jnp.maximum(-diff, 0.0)[None] * lgb))
    col = jnp.broadcast_to(idx[None, :, None], (RET_HEADS, CHUNK, RET_DK))
    xif = jnp.exp((col + 1.0) * lgf)
    xib = jnp.exp((CHUNK - col) * lgb)
    zf = jnp.exp((CHUNK - 1.0 - col) * lgf)
    zb = jnp.exp(col * lgb)
    decf = jnp.exp(CHUNK * lg[0])
    decb = jnp.exp(CHUNK * lg[1])
    return dmat, xif, xib, zf, zb, decf, decb


def kernel(x, g_pre_mix, w_in, ret_decay_logit, sgu_ln_g, sgu_ln_b, w_spatial, b_spatial,
           w_ret_o, w_sgu_o, w_out, g_post_mix, g_pre_ffn, w_up, conv_w, conv_b, w_down,
           g_post_ffn):
    B, S, D = x.shape
    assert D == D_MODEL and S % TOKEN_TILE == 0 and S % FFN_TILE == 0
    row = lambda a: a.astype(F32).reshape(1, -1)
    cos_t, sin_t, kcos_t, ksin_t = _rotary_tables(S)
    dmat, xif, xib, zf, zb, decf, decb = _decay_tables(ret_decay_logit)
    bsp = jnp.broadcast_to(b_spatial.astype(F32)[:, :, None], (SGU_GROUPS, CHUNK, SGU_GW))

    q, k, v, gs, u, sv, sr, ss, rb, w_up_bf16, w_down_bf16 = _inproj_call(
        x, row(g_pre_mix), w_in.astype(BF16), cos_t, sin_t, kcos_t, ksin_t,
        row(sgu_ln_g), row(sgu_ln_b), zb, decb, w_up.astype(F32), w_down.astype(F32))
    x1, h2 = _mixer_call(
        x, q, k, v, gs, u, sv, sr, ss, rb, decf, dmat, xif, xib, zf,
        w_spatial.astype(BF16), bsp, w_ret_o.astype(BF16), w_sgu_o.astype(BF16),
        w_out.astype(BF16), row(g_post_mix), row(g_pre_ffn))
    gate_half = jnp.concatenate([jnp.ones((D_FF,), F32), jnp.full((D_FF,), 0.5, F32)])[None, :]
    return _ffn_call(x1, h2, w_up_bf16, conv_w.astype(F32) * gate_half,
                     row(conv_b) * gate_half, w_down_bf16, row(g_post_ffn))
```
